```python
import math
import jax
import jax.numpy as jnp
from jax import lax
import numpy as np

D_MODEL = 1024
BATCH = 8
SEQ = 2048
DEPTH = 1
DEC_BATCH = 16
DEC_SEQ = 4096
PAST_LEN = 128

HEAD_DIM = 64
N_ATTN_HEADS = 8
ATTN_WIDTH = N_ATTN_HEADS * HEAD_DIM
HYENA_WIDTH = D_MODEL - ATTN_WIDTH
HYENA_ORDER = 2
SHORT_CONV = 3
FILTER_EMB = 33
FILTER_BANDS = (FILTER_EMB - 1) // 2
FILTER_HIDDEN = 64
FAST_DECAY_PCT = 0.3
SLOW_DECAY_PCT = 1.5
DECAY_TARGET = 1e-2
DILATED_PATTERNS = ((128, 1), (512, 4), (2048, 16))
N_EXPERTS = 16
EC_CAPACITY = 2
D_EXPERT = 2816
NORM_EPS = 1e-6
MASK_VALUE = -1e30
IN_WIDTH = 3 * ATTN_WIDTH + 3 * HYENA_WIDTH

kernel_name = 'hymba_longnet_hyena_ec_encoder'


def rms_norm(x, g):
    xf = x.astype(jnp.float32)
    y = xf * lax.rsqrt(jnp.mean(xf * xf, axis=-1, keepdims=True) + NORM_EPS)
    return (y * g.astype(jnp.float32)).astype(x.dtype)


def alibi_slopes(n_heads):
    return np.array([2.0 ** (-8.0 * (h + 1) / n_heads) for h in range(n_heads)], dtype=np.float32)


def dilated_window_attention(q, k, v, window, dilation, slopes):
    B, S, H, Dh = q.shape
    radius = window // (2 * dilation)
    blk = radius
    n = S // dilation
    nb = -(-n // blk)
    n_pad = nb * blk

    def residue_split(a):
        return a.reshape(B, n, dilation, H, Dh).transpose(0, 2, 1, 3, 4)

    def key_windows(a):
        ap = jnp.pad(a, ((0, 0), (0, 0), (blk, n_pad - n + blk), (0, 0), (0, 0)))
        ap = ap.reshape(B, dilation, nb + 2, blk, H, Dh)
        return jnp.concatenate([ap[:, :, 0:nb], ap[:, :, 1:nb + 1], ap[:, :, 2:nb + 2]], axis=3)

    qr = residue_split(q)
    qb = jnp.pad(qr, ((0, 0), (0, 0), (0, n_pad - n), (0, 0), (0, 0))).reshape(B, dilation, nb, blk, H, Dh)
    kb = key_windows(residue_split(k))
    vb = key_windows(residue_split(v))

    s = jnp.einsum('brnqhc,brnkhc->brnhqk', qb, kb).astype(jnp.float32)
    qi = jnp.arange(blk)[:, None]
    kj = jnp.arange(3 * blk)[None, :]
    rel = kj - qi - blk
    j_abs = jnp.arange(nb)[:, None, None] * blk + kj[None] - blk
    valid = (jnp.abs(rel)[None] <= radius) & (j_abs >= 0) & (j_abs < n)
    bias = -jnp.asarray(slopes)[:, None, None] * (dilation * jnp.abs(rel)).astype(jnp.float32)[None]
    s = jnp.where(valid[None, None, :, None], s + bias[None, None, None], MASK_VALUE)
    m = jnp.max(s, axis=-1, keepdims=True)
    p = jnp.exp(s - m)
    l = jnp.sum(p, axis=-1, keepdims=True)
    o = jnp.einsum('brnhqk,brnkhc->brnqhc', p.astype(v.dtype), vb).astype(jnp.float32)
    denom = l[..., 0].transpose(0, 1, 2, 4, 3)
    lse = (m[..., 0] + jnp.log(l[..., 0])).transpose(0, 1, 2, 4, 3)
    o = o / denom[..., None]
    o = o.reshape(B, dilation, n_pad, H, Dh)[:, :, :n].transpose(0, 2, 1, 3, 4).reshape(B, S, H, Dh)
    lse = lse.reshape(B, dilation, n_pad, H)[:, :, :n].transpose(0, 2, 1, 3).reshape(B, S, H)
    return o, lse


def dilated_attention_mixture(q, k, v, slopes):
    outs, lses = [], []
    for window, dilation in DILATED_PATTERNS:
        o, lse = dilated_window_attention(q, k, v, window, dilation, slopes)
        outs.append(o)
        lses.append(lse)
    wts = jax.nn.softmax(jnp.stack(lses, axis=0), axis=0)
    return jnp.sum(wts[..., None] * jnp.stack(outs, axis=0), axis=0)


def short_conv(u, w, b):
    L = u.shape[1]
    half = SHORT_CONV // 2
    up = jnp.pad(u, ((0, 0), (half, half), (0, 0)))
    y = up[:, 0:L] * w[0]
    for i in range(1, SHORT_CONV):
        y = y + up[:, i:i + L] * w[i]
    return y + b


def hyena_filters(L, w1, b1, w2, b2, w3, b3, w4, freq):
    f32 = lambda a: a.astype(jnp.float32)
    t = jnp.linspace(0.0, 1.0, L, dtype=jnp.float32)[:, None]
    wpos = 2.0 * math.pi * jnp.arange(L, dtype=jnp.float32)[:, None] / L
    bands = jnp.linspace(1e-4, FILTER_BANDS - 1, FILTER_BANDS, dtype=jnp.float32)[None]
    feats = jnp.concatenate([t, jnp.cos(bands * wpos), -jnp.sin(bands * wpos)], axis=-1)
    fr = f32(freq)
    h = jnp.sin(fr * (feats @ f32(w1) + f32(b1)))
    h = jnp.sin(fr * (h @ f32(w2) + f32(b2)))
    h = jnp.sin(fr * (h @ f32(w3) + f32(b3)))
    h = (h @ f32(w4)).reshape(L, HYENA_ORDER, 2, HYENA_WIDTH)
    min_decay = math.log(DECAY_TARGET) / FAST_DECAY_PCT
    max_decay = math.log(DECAY_TARGET) / SLOW_DECAY_PCT
    deltas = jnp.linspace(min_decay, max_decay, HYENA_WIDTH, dtype=jnp.float32)
    decay = jnp.exp(-t * jnp.abs(deltas))
    return h * decay[:, None, None, :]


def bidirectional_long_conv(u, h_fwd, h_bwd, skip):
    L = u.shape[1]
    kern = jnp.concatenate([h_fwd, jnp.zeros_like(h_fwd[:1]), h_bwd[:0:-1]], axis=0)
    uf = u.astype(jnp.float32)
    U = jnp.fft.rfft(uf, n=2 * L, axis=1)
    K = jnp.fft.rfft(kern, n=2 * L, axis=0)
    y = jnp.fft.irfft(U * K[None], n=2 * L, axis=1)[:, :L]
    return y + uf * skip.astype(jnp.float32)


def hyena_operator(u, sw, sb, w1, b1, w2, b2, w3, b3, w4, freq, skip):
    L = u.shape[1]
    uc = short_conv(u, sw, sb)
    v, x1, x2 = jnp.split(uc, 3, axis=-1)
    filt = hyena_filters(L, w1, b1, w2, b2, w3, b3, w4, freq)
    z = x1.astype(jnp.float32) * bidirectional_long_conv(v, filt[:, 0, 0], filt[:, 0, 1], skip[0])
    return x2.astype(jnp.float32) * bidirectional_long_conv(z, filt[:, 1, 0], filt[:, 1, 1], skip[1])


def expert_choice_ffn(x, w_router, w_gate, w_up, w_down):
    B, S, D = x.shape
    N = B * S
    cap = EC_CAPACITY * N // N_EXPERTS
    xt = x.reshape(N, D)
    affinity = jax.nn.softmax((xt @ w_router).astype(jnp.float32), axis=-1)
    gates, idx = lax.top_k(affinity.T, cap)

    def one_expert(args):
        ids, g, wg, wu, wd = args
        xe = xt[ids]
        hdn = jax.nn.silu(xe @ wg) * (xe @ wu)
        return (hdn @ wd) * g[:, None].astype(x.dtype)

    ye = lax.map(one_expert, (idx, gates, w_gate, w_up, w_down))
    out = jnp.zeros_like(xt).at[idx.reshape(-1)].add(ye.reshape(-1, D))
    return out.reshape(B, S, D)


def encoder_trunk(x, norm_mix_g, w_in, short_w, short_b, filt_w1, filt_b1, filt_w2, filt_b2,
                  filt_w3, filt_b3, filt_w4, filt_freq, hyena_skip, attn_out_g, hyena_out_g,
                  w_out, norm_ffn_g, w_router, w_gate, w_up, w_down, final_g):
    B, S, _ = x.shape
    slopes = alibi_slopes(N_ATTN_HEADS)
    for l in range(DEPTH):
        h = rms_norm(x, norm_mix_g[l])
        proj = h @ w_in[l]
        q, k, v, u = jnp.split(proj, [ATTN_WIDTH, 2 * ATTN_WIDTH, 3 * ATTN_WIDTH], axis=-1)
        q = q.reshape(B, S, N_ATTN_HEADS, HEAD_DIM) * (HEAD_DIM ** -0.5)
        k = k.reshape(B, S, N_ATTN_HEADS, HEAD_DIM)
        v = v.reshape(B, S, N_ATTN_HEADS, HEAD_DIM)
        attn = dilated_attention_mixture(q, k, v, slopes).reshape(B, S, ATTN_WIDTH)
        hy = hyena_operator(u, short_w[l], short_b[l], filt_w1[l], filt_b1[l], filt_w2[l], filt_b2[l],
                            filt_w3[l], filt_b3[l], filt_w4[l], filt_freq[l], hyena_skip[l])
        mixed = jnp.concatenate([rms_norm(attn, attn_out_g[l]), rms_norm(hy, hyena_out_g[l])], axis=-1).astype(x.dtype)
        x = x + mixed @ w_out[l]
        x = x + expert_choice_ffn(rms_norm(x, norm_ffn_g[l]), w_router[l], w_gate[l], w_up[l], w_down[l])
    return rms_norm(x, final_g)


def setup_inputs(seed: int = 0) -> dict:
    key = jax.random.key(seed)
    ks = jax.random.split(key, 24)
    nrm = lambda k, shape, scale: jax.random.normal(k, shape, jnp.float32) * scale
    gain = lambda k, shape: 1.0 + 0.02 * jax.random.normal(k, shape, jnp.float32)
    return {
        'x_prompt': nrm(ks[0], (BATCH, SEQ, D_MODEL), 1.0),
        'x_sample': nrm(ks[1], (DEC_BATCH, DEC_SEQ, D_MODEL), 1.0),
        'norm_mix_g': gain(ks[2], (DEPTH, D_MODEL)),
        'w_in': nrm(ks[3], (DEPTH, D_MODEL, IN_WIDTH), D_MODEL ** -0.5),
        'short_w': nrm(ks[4], (DEPTH, SHORT_CONV, 3 * HYENA_WIDTH), SHORT_CONV ** -0.5),
        'short_b': nrm(ks[5], (DEPTH, 3 * HYENA_WIDTH), 0.02),
        'filt_w1': nrm(ks[6], (DEPTH, FILTER_EMB, FILTER_HIDDEN), FILTER_EMB ** -0.5),
        'filt_b1': nrm(ks[7], (DEPTH, FILTER_HIDDEN), 0.1),
        'filt_w2': nrm(ks[8], (DEPTH, FILTER_HIDDEN, FILTER_HIDDEN), FILTER_HIDDEN ** -0.5),
        'filt_b2': nrm(ks[9], (DEPTH, FILTER_HIDDEN), 0.1),
        'filt_w3': nrm(ks[10], (DEPTH, FILTER_HIDDEN, FILTER_HIDDEN), FILTER_HIDDEN ** -0.5),
        'filt_b3': nrm(ks[11], (DEPTH, FILTER_HIDDEN), 0.1),
        'filt_w4': nrm(ks[12], (DEPTH, FILTER_HIDDEN, HYENA_ORDER * 2 * HYENA_WIDTH), 0.1 * FILTER_HIDDEN ** -0.5),
        'filt_freq': gain(ks[13], (DEPTH, FILTER_HIDDEN)),
        'hyena_skip': nrm(ks[14], (DEPTH, HYENA_ORDER, HYENA_WIDTH), 1.0),
        'attn_out_g': gain(ks[15], (DEPTH, ATTN_WIDTH)),
        'hyena_out_g': gain(ks[16], (DEPTH, HYENA_WIDTH)),
        'w_out': nrm(ks[17], (DEPTH, D_MODEL, D_MODEL), D_MODEL ** -0.5),
        'norm_ffn_g': gain(ks[18], (DEPTH, D_MODEL)),
        'w_router': nrm(ks[19], (DEPTH, D_MODEL, N_EXPERTS), D_MODEL ** -0.5),
        'w_gate': nrm(ks[20], (DEPTH, N_EXPERTS, D_MODEL, D_EXPERT), D_MODEL ** -0.5),
        'w_up': nrm(ks[21], (DEPTH, N_EXPERTS, D_MODEL, D_EXPERT), D_MODEL ** -0.5),
        'w_down': nrm(ks[22], (DEPTH, N_EXPERTS, D_EXPERT, D_MODEL), D_EXPERT ** -0.5),
        'final_g': gain(ks[23], (D_MODEL,)),
    }


def reference(x_prompt, x_sample, norm_mix_g, w_in, short_w, short_b, filt_w1, filt_b1, filt_w2,
              filt_b2, filt_w3, filt_b3, filt_w4, filt_freq, hyena_skip, attn_out_g, hyena_out_g,
              w_out, norm_ffn_g, w_router, w_gate, w_up, w_down, final_g):
    params = (norm_mix_g, w_in, short_w, short_b, filt_w1, filt_b1, filt_w2, filt_b2, filt_w3,
              filt_b3, filt_w4, filt_freq, hyena_skip, attn_out_g, hyena_out_g, w_out, norm_ffn_g,
              w_router, w_gate, w_up, w_down, final_g)
    y_prompt = encoder_trunk(x_prompt, *params)
    y_sample = encoder_trunk(x_sample, *params)
    return (y_prompt, y_sample)
```

```python
import math
from functools import partial

import jax
import jax.numpy as jnp
import numpy as np
from jax import lax
from jax.experimental import pallas as pl
from jax.experimental.pallas import tpu as pltpu

D_MODEL = 1024
HEAD_DIM = 64
N_ATTN_HEADS = 8
ATTN_WIDTH = N_ATTN_HEADS * HEAD_DIM
HYENA_WIDTH = D_MODEL - ATTN_WIDTH
HYENA_ORDER = 2
SHORT_CONV = 3
FILTER_EMB = 33
FILTER_BANDS = (FILTER_EMB - 1) // 2
FAST_DECAY_PCT = 0.3
SLOW_DECAY_PCT = 1.5
DECAY_TARGET = 1e-2
DILATED_PATTERNS = ((128, 1), (512, 4), (2048, 16))
N_EXPERTS = 16
EC_CAPACITY = 2
D_EXPERT = 2816
NORM_EPS = 1e-6
MASK_VALUE = -1e30

VMEM_LIMIT_BYTES = 56 * 1024 * 1024

FFN_TOKEN_TILE = 2048
FFN_HIDDEN_TILE = 256


def _ffn_kernel(x_ref, gate_ref, wg_ref, wu_ref, wd_ref, o_ref):
    f = pl.program_id(2)
    x = x_ref[0]
    a = jnp.dot(x, wg_ref[0], preferred_element_type=jnp.float32)
    b = jnp.dot(x, wu_ref[0], preferred_element_type=jnp.float32)
    h = (a * jax.nn.sigmoid(a) * b).astype(jnp.bfloat16)
    y = jnp.dot(h, wd_ref[0], preferred_element_type=jnp.float32)

    @pl.when(f == 0)
    def _():
        o_ref[0] = y

    @pl.when(f > 0)
    def _():
        o_ref[0] += y

    @pl.when(f == pl.num_programs(2) - 1)
    def _():
        o_ref[0] *= gate_ref[0]


def expert_ffn(xe, gates, wg, wu, wd):
    E, C, D = xe.shape
    F = wg.shape[-1]
    tm = min(FFN_TOKEN_TILE, C)
    tf = FFN_HIDDEN_TILE
    assert C % tm == 0 and F % tf == 0
    return pl.pallas_call(
        _ffn_kernel,
        grid=(E, C // tm, F // tf),
        in_specs=[
            pl.BlockSpec((1, tm, D), lambda e, i, f: (e, i, 0)),
            pl.BlockSpec((1, tm, 1), lambda e, i, f: (e, i, 0)),
            pl.BlockSpec((1, D, tf), lambda e, i, f: (e, 0, f)),
            pl.BlockSpec((1, D, tf), lambda e, i, f: (e, 0, f)),
            pl.BlockSpec((1, tf, D), lambda e, i, f: (e, f, 0)),
        ],
        out_specs=pl.BlockSpec((1, tm, D), lambda e, i, f: (e, i, 0)),
        out_shape=jax.ShapeDtypeStruct((E, C, D), jnp.float32),
        compiler_params=pltpu.CompilerParams(
            dimension_semantics=("parallel", "parallel", "arbitrary"),
            vmem_limit_bytes=VMEM_LIMIT_BYTES,
        ),
        name="moe_ffn",
    )(xe, gates, wg, wu, wd)


def rms_norm(x, g):
    xf = x.astype(jnp.float32)
    y = xf * lax.rsqrt(jnp.mean(xf * xf, axis=-1, keepdims=True) + NORM_EPS)
    return (y * g.astype(jnp.float32)).astype(x.dtype)


def alibi_slopes(n_heads):
    return np.array([2.0 ** (-8.0 * (h + 1) / n_heads) for h in range(n_heads)], dtype=np.float32)


def dilated_window_attention(q, k, v, window, dilation, slopes):
    B, S, H, Dh = q.shape
    radius = window // (2 * dilation)
    blk = radius
    n = S // dilation
    nb = -(-n // blk)
    n_pad = nb * blk

    def residue_split(a):
        return a.reshape(B, n, dilation, H, Dh).transpose(0, 2, 1, 3, 4)

    def key_windows(a):
        ap = jnp.pad(a, ((0, 0), (0, 0), (blk, n_pad - n + blk), (0, 0), (0, 0)))
        ap = ap.reshape(B, dilation, nb + 2, blk, H, Dh)
        return jnp.concatenate([ap[:, :, 0:nb], ap[:, :, 1:nb + 1], ap[:, :, 2:nb + 2]], axis=3)

    qr = residue_split(q)
    qb = jnp.pad(qr, ((0, 0), (0, 0), (0, n_pad - n), (0, 0), (0, 0))).reshape(B, dilation, nb, blk, H, Dh)
    kb = key_windows(residue_split(k))
    vb = key_windows(residue_split(v))

    s = jnp.einsum('brnqhc,brnkhc->brnhqk', qb, kb).astype(jnp.float32)
    qi = jnp.arange(blk)[:, None]
    kj = jnp.arange(3 * blk)[None, :]
    rel = kj - qi - blk
    j_abs = jnp.arange(nb)[:, None, None] * blk + kj[None] - blk
    valid = (jnp.abs(rel)[None] <= radius) & (j_abs >= 0) & (j_abs < n)
    bias = -jnp.asarray(slopes)[:, None, None] * (dilation * jnp.abs(rel)).astype(jnp.float32)[None]
    s = jnp.where(valid[None, None, :, None], s + bias[None, None, None], MASK_VALUE)
    m = jnp.max(s, axis=-1, keepdims=True)
    p = jnp.exp(s - m)
    l = jnp.sum(p, axis=-1, keepdims=True)
    o = jnp.einsum('brnhqk,brnkhc->brnqhc', p.astype(v.dtype), vb).astype(jnp.float32)
    denom = l[..., 0].transpose(0, 1, 2, 4, 3)
    lse = (m[..., 0] + jnp.log(l[..., 0])).transpose(0, 1, 2, 4, 3)
    o = o / denom[..., None]
    o = o.reshape(B, dilation, n_pad, H, Dh)[:, :, :n].transpose(0, 2, 1, 3, 4).reshape(B, S, H, Dh)
    lse = lse.reshape(B, dilation, n_pad, H)[:, :, :n].transpose(0, 2, 1, 3).reshape(B, S, H)
    return o, lse


def dilated_attention_mixture(q, k, v, slopes):
    outs, lses = [], []
    for window, dilation in DILATED_PATTERNS:
        o, lse = dilated_window_attention(q, k, v, window, dilation, slopes)
        outs.append(o)
        lses.append(lse)
    wts = jax.nn.softmax(jnp.stack(lses, axis=0), axis=0)
    return jnp.sum(wts[..., None] * jnp.stack(outs, axis=0), axis=0)


def short_conv(u, w, b):
    L = u.shape[1]
    half = SHORT_CONV // 2
    up = jnp.pad(u, ((0, 0), (half, half), (0, 0)))
    y = up[:, 0:L] * w[0]
    for i in range(1, SHORT_CONV):
        y = y + up[:, i:i + L] * w[i]
    return y + b


def hyena_filters(L, w1, b1, w2, b2, w3, b3, w4, freq):
    f32 = lambda a: a.astype(jnp.float32)
    t = jnp.linspace(0.0, 1.0, L, dtype=jnp.float32)[:, None]
    wpos = 2.0 * math.pi * jnp.arange(L, dtype=jnp.float32)[:, None] / L
    bands = jnp.linspace(1e-4, FILTER_BANDS - 1, FILTER_BANDS, dtype=jnp.float32)[None]
    feats = jnp.concatenate([t, jnp.cos(bands * wpos), -jnp.sin(bands * wpos)], axis=-1)
    fr = f32(freq)
    h = jnp.sin(fr * (feats @ f32(w1) + f32(b1)))
    h = jnp.sin(fr * (h @ f32(w2) + f32(b2)))
    h = jnp.sin(fr * (h @ f32(w3) + f32(b3)))
    h = (h @ f32(w4)).reshape(L, HYENA_ORDER, 2, HYENA_WIDTH)
    min_decay = math.log(DECAY_TARGET) / FAST_DECAY_PCT
    max_decay = math.log(DECAY_TARGET) / SLOW_DECAY_PCT
    deltas = jnp.linspace(min_decay, max_decay, HYENA_WIDTH, dtype=jnp.float32)
    decay = jnp.exp(-t * jnp.abs(deltas))
    return h * decay[:, None, None, :]


def bidirectional_long_conv(u, h_fwd, h_bwd, skip):
    L = u.shape[1]
    kern = jnp.concatenate([h_fwd, jnp.zeros_like(h_fwd[:1]), h_bwd[:0:-1]], axis=0)
    uf = u.astype(jnp.float32)
    U = jnp.fft.rfft(uf, n=2 * L, axis=1)
    K = jnp.fft.rfft(kern, n=2 * L, axis=0)
    y = jnp.fft.irfft(U * K[None], n=2 * L, axis=1)[:, :L]
    return y + uf * skip.astype(jnp.float32)


def hyena_operator(u, sw, sb, w1, b1, w2, b2, w3, b3, w4, freq, skip):
    L = u.shape[1]
    uc = short_conv(u, sw, sb)
    v, x1, x2 = jnp.split(uc, 3, axis=-1)
    filt = hyena_filters(L, w1, b1, w2, b2, w3, b3, w4, freq)
    z = x1.astype(jnp.float32) * bidirectional_long_conv(v, filt[:, 0, 0], filt[:, 0, 1], skip[0])
    return x2.astype(jnp.float32) * bidirectional_long_conv(z, filt[:, 1, 0], filt[:, 1, 1], skip[1])


def expert_choice_ffn(x, w_router, wg, wu, wd):
    B, S, D = x.shape
    N = B * S
    cap = EC_CAPACITY * N // N_EXPERTS
    xt = x.reshape(N, D)
    affinity = jax.nn.softmax((xt @ w_router).astype(jnp.float32), axis=-1)
    gates, idx = lax.top_k(affinity.T, cap)
    xe = xt.astype(jnp.bfloat16)[idx]
    ye = expert_ffn(xe, gates[..., None], wg, wu, wd)
    out = jnp.zeros_like(xt).at[idx.reshape(-1)].add(ye.reshape(-1, D))
    return out.reshape(B, S, D)


def encoder_trunk(x, norm_mix_g, w_in, short_w, short_b, filt_w1, filt_b1, filt_w2, filt_b2,
                  filt_w3, filt_b3, filt_w4, filt_freq, hyena_skip, attn_out_g, hyena_out_g,
                  w_out, norm_ffn_g, w_router, wg, wu, wd, final_g):
    B, S, _ = x.shape
    slopes = alibi_slopes(N_ATTN_HEADS)
    h = rms_norm(x, norm_mix_g[0])
    proj = h @ w_in[0]
    q, k, v, u = jnp.split(proj, [ATTN_WIDTH, 2 * ATTN_WIDTH, 3 * ATTN_WIDTH], axis=-1)
    q = q.reshape(B, S, N_ATTN_HEADS, HEAD_DIM) * (HEAD_DIM ** -0.5)
    k = k.reshape(B, S, N_ATTN_HEADS, HEAD_DIM)
    v = v.reshape(B, S, N_ATTN_HEADS, HEAD_DIM)
    attn = dilated_attention_mixture(q, k, v, slopes).reshape(B, S, ATTN_WIDTH)
    hy = hyena_operator(u, short_w[0], short_b[0], filt_w1[0], filt_b1[0], filt_w2[0], filt_b2[0],
                        filt_w3[0], filt_b3[0], filt_w4[0], filt_freq[0], hyena_skip[0])
    mixed = jnp.concatenate([rms_norm(attn, attn_out_g[0]), rms_norm(hy, hyena_out_g[0])], axis=-1).astype(x.dtype)
    x = x + mixed @ w_out[0]
    x = x + expert_choice_ffn(rms_norm(x, norm_ffn_g[0]), w_router[0], wg, wu, wd)
    return rms_norm(x, final_g)


def kernel(x_prompt, x_sample, norm_mix_g, w_in, short_w, short_b, filt_w1, filt_b1, filt_w2, filt_b2, filt_w3, filt_b3, filt_w4, filt_freq, hyena_skip, attn_out_g, hyena_out_g, w_out, norm_ffn_g, w_router, w_gate, w_up, w_down, final_g):
    wg = w_gate[0].astype(jnp.bfloat16)
    wu = w_up[0].astype(jnp.bfloat16)
    wd = w_down[0].astype(jnp.bfloat16)
    params = (norm_mix_g, w_in, short_w, short_b, filt_w1, filt_b1, filt_w2, filt_b2, filt_w3,
              filt_b3, filt_w4, filt_freq, hyena_skip, attn_out_g, hyena_out_g, w_out, norm_ffn_g,
              w_router, wg, wu, wd, final_g)
    y_prompt = encoder_trunk(x_prompt, *params)
    y_sample = encoder_trunk(x_sample, *params)
    return (y_prompt, y_sample)
```

```python
import math
from functools import partial

import jax
import jax.numpy as jnp
import numpy as np
from jax import lax
from jax.experimental import pallas as pl
from jax.experimental.pallas import tpu as pltpu

D_MODEL = 1024
HEAD_DIM = 64
N_ATTN_HEADS = 8
ATTN_WIDTH = N_ATTN_HEADS * HEAD_DIM
HYENA_WIDTH = D_MODEL - ATTN_WIDTH
HYENA_ORDER = 2
SHORT_CONV = 3
FILTER_EMB = 33
FILTER_BANDS = (FILTER_EMB - 1) // 2
FAST_DECAY_PCT = 0.3
SLOW_DECAY_PCT = 1.5
DECAY_TARGET = 1e-2
DILATED_PATTERNS = ((128, 1), (512, 4), (2048, 16))
N_EXPERTS = 16
EC_CAPACITY = 2
D_EXPERT = 2816
NORM_EPS = 1e-6
MASK_VALUE = -1e30

LANES = 128
VMEM_LIMIT_BYTES = 56 * 1024 * 1024

ROW_TILE = 512
FFN_TOKEN_TILE = 2048
FFN_HIDDEN_TILE = 256
ATTN_Q_BLOCK = 512
ATTN_Q_TILE = 128


def _compiler_params(semantics):
    return pltpu.CompilerParams(dimension_semantics=semantics, vmem_limit_bytes=VMEM_LIMIT_BYTES)


def _rms_scale(x):
    return lax.rsqrt(jnp.mean(x * x, axis=-1, keepdims=True) + NORM_EPS)


def _in_proj_kernel(x_ref, g_ref, w_ref, qkv_ref, u_ref):
    x = x_ref[...]
    h = (x * _rms_scale(x) * g_ref[...]).astype(jnp.bfloat16)
    a = ATTN_WIDTH
    qkv = jnp.dot(h, w_ref[:, :3 * a], preferred_element_type=jnp.float32)
    qkv_ref[:, :a] = (qkv[:, :a] * (HEAD_DIM ** -0.5)).astype(jnp.bfloat16)
    qkv_ref[:, a:] = qkv[:, a:].astype(jnp.bfloat16)
    u_ref[...] = jnp.dot(h, w_ref[:, 3 * a:], preferred_element_type=jnp.float32)


def in_proj(x, g, w):
    B, L, D = x.shape
    N = B * L
    tm = min(ROW_TILE, N)
    assert N % tm == 0
    a3, h3 = 3 * ATTN_WIDTH, 3 * HYENA_WIDTH
    qkv, u = pl.pallas_call(
        _in_proj_kernel,
        grid=(N // tm,),
        in_specs=[
            pl.BlockSpec((tm, D), lambda i: (i, 0)),
            pl.BlockSpec((1, D), lambda i: (0, 0)),
            pl.BlockSpec((D, a3 + h3), lambda i: (0, 0)),
        ],
        out_specs=[pl.BlockSpec((tm, a3), lambda i: (i, 0)), pl.BlockSpec((tm, h3), lambda i: (i, 0))],
        out_shape=[jax.ShapeDtypeStruct((N, a3), jnp.bfloat16), jax.ShapeDtypeStruct((N, h3), jnp.float32)],
        compiler_params=_compiler_params(("parallel",)),
        name="in_proj",
    )(x.reshape(N, D), g, w)
    return qkv.reshape(B, L, a3), u.reshape(B, L, h3)


def _alibi_slope(h):
    return 2.0 ** (-8.0 * (h + 1) / N_ATTN_HEADS)


def _attn_kernel(*refs, dilation, radius, first, last, n, tq, tk):
    if first:
        q_ref, k_ref, v_ref, o_ref, lse_ref = refs
    elif last:
        q_ref, k_ref, v_ref, op_ref, lsep_ref, g_ref, out_ref = refs
    else:
        q_ref, k_ref, v_ref, op_ref, lsep_ref, o_ref, lse_ref = refs
    tqo = q_ref.shape[1]
    q_block0 = pl.program_id(2) * tqo
    n_pairs = ATTN_WIDTH // LANES

    lane = lax.broadcasted_iota(jnp.int32, (tq, LANES), 1)
    low = lane < HEAD_DIM
    row2 = lax.broadcasted_iota(jnp.int32, (2 * tq, tk), 0)
    col2 = lax.broadcasted_iota(jnp.int32, (2 * tq, tk), 1)
    second = row2 >= tq
    base_rel = col2 - jnp.where(second, row2 - tq, row2)
    second_col = lax.broadcasted_iota(jnp.int32, (2 * tq, 1), 0) >= tq

    def tile(j, carry):
        r0 = pl.multiple_of(j * tq, tq)
        q0 = q_block0 + r0
        ks = pl.multiple_of(jnp.clip(q0 - radius, 0, n - tk), 16)
        absrel = jnp.abs(base_rel - (q0 - ks))
        valid = absrel <= radius
        dist = (dilation * absrel).astype(jnp.float32)
        pair_out = []
        if not first:
            lse_prev = lsep_ref[0, pl.ds(r0, tq), :]
        lse_tile = jnp.zeros((tq, LANES), jnp.float32)
        for hp in range(n_pairs):
            lanes = slice(hp * LANES, (hp + 1) * LANES)
            q = q_ref[0, pl.ds(r0, tq), lanes]
            kw = k_ref[0, pl.ds(ks, tk), lanes]
            vw = v_ref[0, pl.ds(ks, tk), lanes]
            zero = jnp.zeros_like(q)
            q2 = jnp.concatenate([jnp.where(low, q, zero), jnp.where(low, zero, q)], axis=0)
            s = lax.dot_general(q2, kw, (((1,), (1,)), ((), ())), preferred_element_type=jnp.float32)
            slope = jnp.where(second_col, _alibi_slope(2 * hp + 1), _alibi_slope(2 * hp))
            s = jnp.where(valid, s - slope * dist, MASK_VALUE)
            m = jnp.max(s, axis=-1, keepdims=True)
            p = jnp.exp(s - m)
            l = jnp.sum(p, axis=-1, keepdims=True)
            o2 = jnp.dot(p.astype(jnp.bfloat16), vw, preferred_element_type=jnp.float32) / l
            lse2 = m + jnp.log(l)
            o = jnp.where(low, o2[:tq], o2[tq:])
            lse_a, lse_b = lse2[:tq], lse2[tq:]
            if not first:
                pa = lse_prev[:, 2 * hp:2 * hp + 1]
                pb = lse_prev[:, 2 * hp + 1:2 * hp + 2]
                na = jnp.logaddexp(pa, lse_a)
                nb = jnp.logaddexp(pb, lse_b)
                w_prev = jnp.where(low, jnp.exp(pa - na), jnp.exp(pb - nb))
                w_cur = jnp.where(low, jnp.exp(lse_a - na), jnp.exp(lse_b - nb))
                o = op_ref[0, pl.ds(r0, tq), lanes] * w_prev + o * w_cur
                lse_a, lse_b = na, nb
            if last:
                pair_out.append(o)
            else:
                o_ref[0, pl.ds(r0, tq), lanes] = o
                lse_tile = jnp.where(lane == 2 * hp, lse_a, lse_tile)
                lse_tile = jnp.where(lane == 2 * hp + 1, lse_b, lse_tile)
        if last:
            ssq = sum(jnp.sum(o * o, axis=-1, keepdims=True) for o in pair_out)
            scale = lax.rsqrt(ssq / ATTN_WIDTH + NORM_EPS)
            for hp, o in enumerate(pair_out):
                lanes = slice(hp * LANES, (hp + 1) * LANES)
                out_ref[0, pl.ds(r0, tq), lanes] = (o * scale * g_ref[:, lanes]).astype(out_ref.dtype)
        else:
            lse_ref[0, pl.ds(r0, tq), :] = lse_tile
        return carry

    lax.fori_loop(0, tqo // tq, tile, 0)


def _attn_branch(qkv, state, g, window, dilation, first, last):
    B, L, _ = qkv.shape
    a = ATTN_WIDTH
    radius = window // (2 * dilation)
    n = L // dilation
    assert L % dilation == 0 and radius % 16 == 0
    tq = min(ATTN_Q_TILE, n)
    tk = min(tq + 2 * radius, n)
    tqo = min(ATTN_Q_BLOCK, n)
    assert n % tqo == 0 and tqo % tq == 0 and n % 16 == 0 and tk % 16 == 0
    grid = (B, dilation, n // tqo)
    qkv_v = qkv.reshape(B, n, dilation * 3 * a)
    q_spec = pl.BlockSpec((1, tqo, a), lambda b, r, i: (b, i, 3 * r))
    k_spec = pl.BlockSpec((1, n, a), lambda b, r, i: (b, 0, 3 * r + 1))
    v_spec = pl.BlockSpec((1, n, a), lambda b, r, i: (b, 0, 3 * r + 2))
    o_spec = pl.BlockSpec((1, tqo, a), lambda b, r, i: (b, i, r))
    lse_spec = pl.BlockSpec((1, tqo, LANES), lambda b, r, i: (b, i, r))
    o_shape = jax.ShapeDtypeStruct((B, n, dilation * a), jnp.float32)
    lse_shape = jax.ShapeDtypeStruct((B, n, dilation * LANES), jnp.float32)
    in_specs = [q_spec, k_spec, v_spec]
    args = [qkv_v, qkv_v, qkv_v]
    aliases = {}
    if not first:
        o_prev, lse_prev = state
        in_specs += [o_spec, lse_spec]
        args += [o_prev.reshape(B, n, dilation * a), lse_prev.reshape(B, n, dilation * LANES)]
    if last:
        in_specs.append(pl.BlockSpec((1, a), lambda b, r, i: (0, 0)))
        args.append(g)
        out_specs = o_spec
        out_shape = jax.ShapeDtypeStruct((B, n, dilation * a), jnp.bfloat16)
    else:
        out_specs = [o_spec, lse_spec]
        out_shape = [o_shape, lse_shape]
        if not first:
            aliases = {3: 0, 4: 1}
    out = pl.pallas_call(
        partial(_attn_kernel, dilation=dilation, radius=radius, first=first, last=last, n=n, tq=tq, tk=tk),
        grid=grid,
        in_specs=in_specs,
        out_specs=out_specs,
        out_shape=out_shape,
        input_output_aliases=aliases,
        compiler_params=_compiler_params(("parallel", "parallel", "arbitrary")),
        name=f"dilated_attn_d{dilation}",
    )(*args)
    if last:
        return out.reshape(B, L, a)
    return out[0].reshape(B, L, a), out[1].reshape(B, L, LANES)


def dilated_attention(qkv, g):
    state = None
    last_idx = len(DILATED_PATTERNS) - 1
    for idx, (window, dilation) in enumerate(DILATED_PATTERNS):
        state = _attn_branch(qkv, state, g, window, dilation, idx == 0, idx == last_idx)
    return state


def _out_proj_kernel(x_ref, attn_ref, hy_ref, hg_ref, w_ref, ng_ref, wr_ref, x1_ref, h_ref, aff_ref):
    a = ATTN_WIDTH
    hy = hy_ref[...]
    hyn = (hy * _rms_scale(hy) * hg_ref[...]).astype(jnp.bfloat16)
    y = jnp.dot(attn_ref[...], w_ref[:a, :], preferred_element_type=jnp.float32)
    y = y + jnp.dot(hyn, w_ref[a:, :], preferred_element_type=jnp.float32)
    x1 = x_ref[...] + y
    x1_ref[...] = x1
    h = x1 * _rms_scale(x1) * ng_ref[...]
    h_ref[...] = h.astype(jnp.bfloat16)
    logits = jnp.dot(h, wr_ref[...], preferred_element_type=jnp.float32, precision=lax.Precision.HIGHEST)
    e = jnp.exp(logits - jnp.max(logits, axis=-1, keepdims=True))
    aff_ref[...] = e / jnp.sum(e, axis=-1, keepdims=True)


def out_proj(x, attn_n, hy, hy_g, w, ffn_g, w_router):
    B, L, D = x.shape
    N = B * L
    tm = min(ROW_TILE, N)
    assert N % tm == 0
    E = w_router.shape[-1]
    row = lambda width: pl.BlockSpec((tm, width), lambda i: (i, 0))
    full = lambda r, c: pl.BlockSpec((r, c), lambda i: (0, 0))
    return pl.pallas_call(
        _out_proj_kernel,
        grid=(N // tm,),
        in_specs=[row(D), row(ATTN_WIDTH), row(HYENA_WIDTH), full(1, HYENA_WIDTH), full(D, D), full(1, D),
                  full(D, E)],
        out_specs=[row(D), row(D), row(E)],
        out_shape=[jax.ShapeDtypeStruct((N, D), jnp.float32), jax.ShapeDtypeStruct((N, D), jnp.bfloat16),
                   jax.ShapeDtypeStruct((N, E), jnp.float32)],
        compiler_params=_compiler_params(("parallel",)),
        name="out_proj",
    )(x.reshape(N, D), attn_n.reshape(N, ATTN_WIDTH), hy.reshape(N, HYENA_WIDTH), hy_g, w, ffn_g, w_router)


def _ffn_kernel(x_ref, gate_ref, wg_ref, wu_ref, wd_ref, o_ref):
    f = pl.program_id(2)
    x = x_ref[0]
    a = jnp.dot(x, wg_ref[0], preferred_element_type=jnp.float32)
    b = jnp.dot(x, wu_ref[0], preferred_element_type=jnp.float32)
    h = (a * jax.nn.sigmoid(a) * b).astype(jnp.bfloat16)
    y = jnp.dot(h, wd_ref[0], preferred_element_type=jnp.float32)

    @pl.when(f == 0)
    def _():
        o_ref[0] = y

    @pl.when(f > 0)
    def _():
        o_ref[0] += y

    @pl.when(f == pl.num_programs(2) - 1)
    def _():
        o_ref[0] *= gate_ref[0]


def expert_ffn(xe, gates, wg, wu, wd):
    E, C, D = xe.shape
    F = wg.shape[-1]
    tm = min(FFN_TOKEN_TILE, C)
    tf = min(FFN_HIDDEN_TILE, F)
    assert C % tm == 0 and F % tf == 0
    return pl.pallas_call(
        _ffn_kernel,
        grid=(E, C // tm, F // tf),
        in_specs=[
            pl.BlockSpec((1, tm, D), lambda e, i, f: (e, i, 0)),
            pl.BlockSpec((1, tm, 1), lambda e, i, f: (e, i, 0)),
            pl.BlockSpec((1, D, tf), lambda e, i, f: (e, 0, f)),
            pl.BlockSpec((1, D, tf), lambda e, i, f: (e, 0, f)),
            pl.BlockSpec((1, tf, D), lambda e, i, f: (e, f, 0)),
        ],
        out_specs=pl.BlockSpec((1, tm, D), lambda e, i, f: (e, i, 0)),
        out_shape=jax.ShapeDtypeStruct((E, C, D), jnp.float32),
        compiler_params=_compiler_params(("parallel", "parallel", "arbitrary")),
        name="moe_ffn",
    )(xe, gates, wg, wu, wd)


def _final_kernel(a_ref, b_ref, g_ref, o_ref):
    x = a_ref[...] + b_ref[...]
    o_ref[...] = x * _rms_scale(x) * g_ref[...]


def final_combine(x1, moe, g):
    N, D = x1.shape
    tm = min(ROW_TILE, N)
    row = pl.BlockSpec((tm, D), lambda i: (i, 0))
    return pl.pallas_call(
        _final_kernel,
        grid=(N // tm,),
        in_specs=[row, row, pl.BlockSpec((1, D), lambda i: (0, 0))],
        out_specs=row,
        out_shape=jax.ShapeDtypeStruct((N, D), jnp.float32),
        compiler_params=_compiler_params(("parallel",)),
        name="final_norm",
    )(x1, moe, g)


def short_conv(u, w, b):
    L = u.shape[1]
    half = SHORT_CONV // 2
    up = jnp.pad(u, ((0, 0), (half, half), (0, 0)))
    y = up[:, 0:L] * w[0]
    for i in range(1, SHORT_CONV):
        y = y + up[:, i:i + L] * w[i]
    return y + b


def hyena_filters(L, w1, b1, w2, b2, w3, b3, w4, freq):
    f32 = lambda a: a.astype(jnp.float32)
    t = jnp.linspace(0.0, 1.0, L, dtype=jnp.float32)[:, None]
    wpos = 2.0 * math.pi * jnp.arange(L, dtype=jnp.float32)[:, None] / L
    bands = jnp.linspace(1e-4, FILTER_BANDS - 1, FILTER_BANDS, dtype=jnp.float32)[None]
    feats = jnp.concatenate([t, jnp.cos(bands * wpos), -jnp.sin(bands * wpos)], axis=-1)
    fr = f32(freq)
    h = jnp.sin(fr * (feats @ f32(w1) + f32(b1)))
    h = jnp.sin(fr * (h @ f32(w2) + f32(b2)))
    h = jnp.sin(fr * (h @ f32(w3) + f32(b3)))
    h = (h @ f32(w4)).reshape(L, HYENA_ORDER, 2, HYENA_WIDTH)
    min_decay = math.log(DECAY_TARGET) / FAST_DECAY_PCT
    max_decay = math.log(DECAY_TARGET) / SLOW_DECAY_PCT
    deltas = jnp.linspace(min_decay, max_decay, HYENA_WIDTH, dtype=jnp.float32)
    decay = jnp.exp(-t * jnp.abs(deltas))
    return h * decay[:, None, None, :]


def bidirectional_long_conv(u, h_fwd, h_bwd, skip):
    L = u.shape[1]
    kern = jnp.concatenate([h_fwd, jnp.zeros_like(h_fwd[:1]), h_bwd[:0:-1]], axis=0)
    uf = u.astype(jnp.float32)
    U = jnp.fft.rfft(uf, n=2 * L, axis=1)
    K = jnp.fft.rfft(kern, n=2 * L, axis=0)
    y = jnp.fft.irfft(U * K[None], n=2 * L, axis=1)[:, :L]
    return y + uf * skip.astype(jnp.float32)


def hyena_operator(u, sw, sb, w1, b1, w2, b2, w3, b3, w4, freq, skip):
    L = u.shape[1]
    uc = short_conv(u, sw, sb)
    v, x1, x2 = jnp.split(uc, 3, axis=-1)
    filt = hyena_filters(L, w1, b1, w2, b2, w3, b3, w4, freq)
    z = x1.astype(jnp.float32) * bidirectional_long_conv(v, filt[:, 0, 0], filt[:, 0, 1], skip[0])
    return x2.astype(jnp.float32) * bidirectional_long_conv(z, filt[:, 1, 0], filt[:, 1, 1], skip[1])


def expert_choice_ffn(h, affinity, wg, wu, wd):
    N, D = h.shape
    cap = EC_CAPACITY * N // N_EXPERTS
    gates, idx = lax.top_k(affinity.T, cap)
    ye = expert_ffn(h[idx], gates[..., None], wg, wu, wd)
    return jnp.zeros((N, D), jnp.float32).at[idx.reshape(-1)].add(ye.reshape(-1, D))


def encoder_trunk(x, p):
    B, L, D = x.shape
    qkv, u = in_proj(x, p["norm_mix_g"], p["w_in"])
    attn_n = dilated_attention(qkv, p["attn_out_g"])
    hy = hyena_operator(u, p["short_w"], p["short_b"], p["filt_w1"], p["filt_b1"], p["filt_w2"], p["filt_b2"],
                        p["filt_w3"], p["filt_b3"], p["filt_w4"], p["filt_freq"], p["hyena_skip"])
    x1, h, affinity = out_proj(x, attn_n, hy, p["hyena_out_g"], p["w_out"], p["norm_ffn_g"], p["w_router"])
    moe = expert_choice_ffn(h, affinity, p["w_gate"], p["w_up"], p["w_down"])
    return final_combine(x1, moe, p["final_g"]).reshape(B, L, D)


def kernel(x_prompt, x_sample, norm_mix_g, w_in, short_w, short_b, filt_w1, filt_b1, filt_w2, filt_b2, filt_w3, filt_b3, filt_w4, filt_freq, hyena_skip, attn_out_g, hyena_out_g, w_out, norm_ffn_g, w_router, w_gate, w_up, w_down, final_g):
    bf16 = jnp.bfloat16
    p = {
        "norm_mix_g": norm_mix_g, "w_in": w_in[0].astype(bf16),
        "short_w": short_w[0], "short_b": short_b[0],
        "filt_w1": filt_w1[0], "filt_b1": filt_b1[0], "filt_w2": filt_w2[0], "filt_b2": filt_b2[0],
        "filt_w3": filt_w3[0], "filt_b3": filt_b3[0], "filt_w4": filt_w4[0], "filt_freq": filt_freq[0],
        "hyena_skip": hyena_skip[0], "attn_out_g": attn_out_g, "hyena_out_g": hyena_out_g,
        "w_out": w_out[0].astype(bf16), "norm_ffn_g": norm_ffn_g, "w_router": w_router[0],
        "w_gate": w_gate[0].astype(bf16), "w_up": w_up[0].astype(bf16), "w_down": w_down[0].astype(bf16),
        "final_g": final_g.reshape(1, D_MODEL),
    }
    return (encoder_trunk(x_prompt, p), encoder_trunk(x_sample, p))
```

```python
import math
from functools import partial

import jax
import jax.numpy as jnp
import numpy as np
from jax import lax
from jax.experimental import pallas as pl
from jax.experimental.pallas import tpu as pltpu

D_MODEL = 1024
HEAD_DIM = 64
N_ATTN_HEADS = 8
ATTN_WIDTH = N_ATTN_HEADS * HEAD_DIM
HYENA_WIDTH = D_MODEL - ATTN_WIDTH
HYENA_ORDER = 2
SHORT_CONV = 3
FILTER_EMB = 33
FILTER_BANDS = (FILTER_EMB - 1) // 2
FAST_DECAY_PCT = 0.3
SLOW_DECAY_PCT = 1.5
DECAY_TARGET = 1e-2
DILATED_PATTERNS = ((128, 1), (512, 4), (2048, 16))
N_EXPERTS = 16
EC_CAPACITY = 2
D_EXPERT = 2816
NORM_EPS = 1e-6
MASK_VALUE = -1e30

LANES = 128
VMEM_LIMIT_BYTES = 56 * 1024 * 1024

ROW_TILE = 512
FFN_TOKEN_TILE = 2048
FFN_HIDDEN_TILE = 256
ATTN_Q_BLOCK = 512
ATTN_Q_TILE = 128


def _compiler_params(semantics):
    return pltpu.CompilerParams(dimension_semantics=semantics, vmem_limit_bytes=VMEM_LIMIT_BYTES)


def _rms_scale(x):
    return lax.rsqrt(jnp.mean(x * x, axis=-1, keepdims=True) + NORM_EPS)


def _in_proj_kernel(x_ref, g_ref, w_ref, qkv_ref, u_ref):
    x = x_ref[...]
    h = (x * _rms_scale(x) * g_ref[...]).astype(jnp.bfloat16)
    a = ATTN_WIDTH
    qkv = jnp.dot(h, w_ref[:, :3 * a], preferred_element_type=jnp.float32)
    qkv_ref[:, :a] = (qkv[:, :a] * (HEAD_DIM ** -0.5)).astype(jnp.bfloat16)
    qkv_ref[:, a:] = qkv[:, a:].astype(jnp.bfloat16)
    u = jnp.dot(h, w_ref[:, 3 * a:], preferred_element_type=jnp.float32)
    for s in range(u_ref.shape[1]):
        u_ref[0, s] = u[:, s * LANES:(s + 1) * LANES]


def in_proj(x, g, w):
    B, L, D = x.shape
    N = B * L
    tm = min(ROW_TILE, L)
    assert L % tm == 0
    nt = L // tm
    a3, h3 = 3 * ATTN_WIDTH, 3 * HYENA_WIDTH
    n_slabs = h3 // LANES
    qkv, u = pl.pallas_call(
        _in_proj_kernel,
        grid=(N // tm,),
        in_specs=[
            pl.BlockSpec((tm, D), lambda i: (i, 0)),
            pl.BlockSpec((1, D), lambda i: (0, 0)),
            pl.BlockSpec((D, a3 + h3), lambda i: (0, 0)),
        ],
        out_specs=[pl.BlockSpec((tm, a3), lambda i: (i, 0)),
                   pl.BlockSpec((1, n_slabs, tm, LANES), lambda i: (i // nt, 0, i % nt, 0))],
        out_shape=[jax.ShapeDtypeStruct((N, a3), jnp.bfloat16),
                   jax.ShapeDtypeStruct((B, n_slabs, L, LANES), jnp.float32)],
        compiler_params=_compiler_params(("parallel",)),
        name="in_proj",
    )(x.reshape(N, D), g, w)
    return qkv.reshape(B, L, a3), u


def _alibi_slope(h):
    return 2.0 ** (-8.0 * (h + 1) / N_ATTN_HEADS)


def _attn_kernel(*refs, dilation, radius, first, last, n, tq, tk):
    if first:
        q_ref, k_ref, v_ref, o_ref, lse_ref = refs
    elif last:
        q_ref, k_ref, v_ref, op_ref, lsep_ref, g_ref, out_ref = refs
    else:
        q_ref, k_ref, v_ref, op_ref, lsep_ref, o_ref, lse_ref = refs
    tqo = q_ref.shape[1]
    q_block0 = pl.program_id(2) * tqo
    n_pairs = ATTN_WIDTH // LANES

    lane = lax.broadcasted_iota(jnp.int32, (tq, LANES), 1)
    low = lane < HEAD_DIM
    row2 = lax.broadcasted_iota(jnp.int32, (2 * tq, tk), 0)
    col2 = lax.broadcasted_iota(jnp.int32, (2 * tq, tk), 1)
    second = row2 >= tq
    base_rel = col2 - jnp.where(second, row2 - tq, row2)
    second_col = lax.broadcasted_iota(jnp.int32, (2 * tq, 1), 0) >= tq

    def tile(j, carry):
        r0 = pl.multiple_of(j * tq, tq)
        q0 = q_block0 + r0
        ks = pl.multiple_of(jnp.clip(q0 - radius, 0, n - tk), 16)
        absrel = jnp.abs(base_rel - (q0 - ks))
        valid = absrel <= radius
        dist = (dilation * absrel).astype(jnp.float32)
        pair_out = []
        if not first:
            lse_prev = lsep_ref[0, pl.ds(r0, tq), :]
        lse_tile = jnp.zeros((tq, LANES), jnp.float32)
        for hp in range(n_pairs):
            lanes = slice(hp * LANES, (hp + 1) * LANES)
            q = q_ref[0, pl.ds(r0, tq), lanes]
            kw = k_ref[0, pl.ds(ks, tk), lanes]
            vw = v_ref[0, pl.ds(ks, tk), lanes]
            zero = jnp.zeros_like(q)
            q2 = jnp.concatenate([jnp.where(low, q, zero), jnp.where(low, zero, q)], axis=0)
            s = lax.dot_general(q2, kw, (((1,), (1,)), ((), ())), preferred_element_type=jnp.float32)
            slope = jnp.where(second_col, _alibi_slope(2 * hp + 1), _alibi_slope(2 * hp))
            s = jnp.where(valid, s - slope * dist, MASK_VALUE)
            m = jnp.max(s, axis=-1, keepdims=True)
            p = jnp.exp(s - m)
            l = jnp.sum(p, axis=-1, keepdims=True)
            o2 = jnp.dot(p.astype(jnp.bfloat16), vw, preferred_element_type=jnp.float32) / l
            lse2 = m + jnp.log(l)
            o = jnp.where(low, o2[:tq], o2[tq:])
            lse_a, lse_b = lse2[:tq], lse2[tq:]
            if not first:
                pa = lse_prev[:, 2 * hp:2 * hp + 1]
                pb = lse_prev[:, 2 * hp + 1:2 * hp + 2]
                na = jnp.logaddexp(pa, lse_a)
                nb = jnp.logaddexp(pb, lse_b)
                w_prev = jnp.where(low, jnp.exp(pa - na), jnp.exp(pb - nb))
                w_cur = jnp.where(low, jnp.exp(lse_a - na), jnp.exp(lse_b - nb))
                o = op_ref[0, pl.ds(r0, tq), lanes] * w_prev + o * w_cur
                lse_a, lse_b = na, nb
            if last:
                pair_out.append(o)
            else:
                o_ref[0, pl.ds(r0, tq), lanes] = o
                lse_tile = jnp.where(lane == 2 * hp, lse_a, lse_tile)
                lse_tile = jnp.where(lane == 2 * hp + 1, lse_b, lse_tile)
        if last:
            ssq = sum(jnp.sum(o * o, axis=-1, keepdims=True) for o in pair_out)
            scale = lax.rsqrt(ssq / ATTN_WIDTH + NORM_EPS)
            for hp, o in enumerate(pair_out):
                lanes = slice(hp * LANES, (hp + 1) * LANES)
                out_ref[0, pl.ds(r0, tq), lanes] = (o * scale * g_ref[:, lanes]).astype(out_ref.dtype)
        else:
            lse_ref[0, pl.ds(r0, tq), :] = lse_tile
        return carry

    lax.fori_loop(0, tqo // tq, tile, 0)


def _attn_branch(qkv, state, g, window, dilation, first, last):
    B, L, _ = qkv.shape
    a = ATTN_WIDTH
    radius = window // (2 * dilation)
    n = L // dilation
    assert L % dilation == 0 and radius % 16 == 0
    tq = min(ATTN_Q_TILE, n)
    tk = min(tq + 2 * radius, n)
    tqo = min(ATTN_Q_BLOCK, n)
    assert n % tqo == 0 and tqo % tq == 0 and n % 16 == 0 and tk % 16 == 0
    grid = (B, dilation, n // tqo)
    qkv_v = qkv.reshape(B, n, dilation * 3 * a)
    q_spec = pl.BlockSpec((1, tqo, a), lambda b, r, i: (b, i, 3 * r))
    k_spec = pl.BlockSpec((1, n, a), lambda b, r, i: (b, 0, 3 * r + 1))
    v_spec = pl.BlockSpec((1, n, a), lambda b, r, i: (b, 0, 3 * r + 2))
    o_spec = pl.BlockSpec((1, tqo, a), lambda b, r, i: (b, i, r))
    lse_spec = pl.BlockSpec((1, tqo, LANES), lambda b, r, i: (b, i, r))
    o_shape = jax.ShapeDtypeStruct((B, n, dilation * a), jnp.float32)
    lse_shape = jax.ShapeDtypeStruct((B, n, dilation * LANES), jnp.float32)
    in_specs = [q_spec, k_spec, v_spec]
    args = [qkv_v, qkv_v, qkv_v]
    aliases = {}
    if not first:
        o_prev, lse_prev = state
        in_specs += [o_spec, lse_spec]
        args += [o_prev.reshape(B, n, dilation * a), lse_prev.reshape(B, n, dilation * LANES)]
    if last:
        in_specs.append(pl.BlockSpec((1, a), lambda b, r, i: (0, 0)))
        args.append(g)
        out_specs = o_spec
        out_shape = jax.ShapeDtypeStruct((B, n, dilation * a), jnp.bfloat16)
    else:
        out_specs = [o_spec, lse_spec]
        out_shape = [o_shape, lse_shape]
        if not first:
            aliases = {3: 0, 4: 1}
    out = pl.pallas_call(
        partial(_attn_kernel, dilation=dilation, radius=radius, first=first, last=last, n=n, tq=tq, tk=tk),
        grid=grid,
        in_specs=in_specs,
        out_specs=out_specs,
        out_shape=out_shape,
        input_output_aliases=aliases,
        compiler_params=_compiler_params(("parallel", "parallel", "arbitrary")),
        name=f"dilated_attn_d{dilation}",
    )(*args)
    if last:
        return out.reshape(B, L, a)
    return out[0].reshape(B, L, a), out[1].reshape(B, L, LANES)


def dilated_attention(qkv, g):
    state = None
    last_idx = len(DILATED_PATTERNS) - 1
    for idx, (window, dilation) in enumerate(DILATED_PATTERNS):
        state = _attn_branch(qkv, state, g, window, dilation, idx == 0, idx == last_idx)
    return state


def _out_proj_kernel(x_ref, attn_ref, hy_ref, hg_ref, w_ref, ng_ref, wr_ref, x1_ref, h_ref, aff_ref):
    a = ATTN_WIDTH
    hy = jnp.concatenate([hy_ref[0, s] for s in range(hy_ref.shape[1])], axis=-1)
    hyn = (hy * _rms_scale(hy) * hg_ref[...]).astype(jnp.bfloat16)
    y = jnp.dot(attn_ref[...], w_ref[:a, :], preferred_element_type=jnp.float32)
    y = y + jnp.dot(hyn, w_ref[a:, :], preferred_element_type=jnp.float32)
    x1 = x_ref[...] + y
    x1_ref[...] = x1
    h = x1 * _rms_scale(x1) * ng_ref[...]
    h_ref[...] = h.astype(jnp.bfloat16)
    logits = jnp.dot(h, wr_ref[...], preferred_element_type=jnp.float32, precision=lax.Precision.HIGHEST)
    e = jnp.exp(logits - jnp.max(logits, axis=-1, keepdims=True))
    aff_ref[...] = e / jnp.sum(e, axis=-1, keepdims=True)


def out_proj(x, attn_n, hy, hy_g, w, ffn_g, w_router):
    B, L, D = x.shape
    N = B * L
    tm = min(ROW_TILE, L)
    assert L % tm == 0
    nt = L // tm
    E = w_router.shape[-1]
    hy_spec = pl.BlockSpec((1, HYENA_WIDTH // LANES, tm, LANES), lambda i: (i // nt, 0, i % nt, 0))
    row = lambda width: pl.BlockSpec((tm, width), lambda i: (i, 0))
    full = lambda r, c: pl.BlockSpec((r, c), lambda i: (0, 0))
    return pl.pallas_call(
        _out_proj_kernel,
        grid=(N // tm,),
        in_specs=[row(D), row(ATTN_WIDTH), hy_spec, full(1, HYENA_WIDTH), full(D, D), full(1, D),
                  full(D, E)],
        out_specs=[row(D), row(D), row(E)],
        out_shape=[jax.ShapeDtypeStruct((N, D), jnp.float32), jax.ShapeDtypeStruct((N, D), jnp.bfloat16),
                   jax.ShapeDtypeStruct((N, E), jnp.float32)],
        compiler_params=_compiler_params(("parallel",)),
        name="out_proj",
    )(x.reshape(N, D), attn_n.reshape(N, ATTN_WIDTH), hy, hy_g, w, ffn_g, w_router)


def _ffn_kernel(x_ref, gate_ref, wg_ref, wu_ref, wd_ref, o_ref):
    f = pl.program_id(2)
    x = x_ref[0]
    a = jnp.dot(x, wg_ref[0], preferred_element_type=jnp.float32)
    b = jnp.dot(x, wu_ref[0], preferred_element_type=jnp.float32)
    h = (a * jax.nn.sigmoid(a) * b).astype(jnp.bfloat16)
    y = jnp.dot(h, wd_ref[0], preferred_element_type=jnp.float32)

    @pl.when(f == 0)
    def _():
        o_ref[0] = y

    @pl.when(f > 0)
    def _():
        o_ref[0] += y

    @pl.when(f == pl.num_programs(2) - 1)
    def _():
        o_ref[0] *= gate_ref[0]


def expert_ffn(xe, gates, wg, wu, wd):
    E, C, D = xe.shape
    F = wg.shape[-1]
    tm = min(FFN_TOKEN_TILE, C)
    tf = min(FFN_HIDDEN_TILE, F)
    assert C % tm == 0 and F % tf == 0
    return pl.pallas_call(
        _ffn_kernel,
        grid=(E, C // tm, F // tf),
        in_specs=[
            pl.BlockSpec((1, tm, D), lambda e, i, f: (e, i, 0)),
            pl.BlockSpec((1, tm, 1), lambda e, i, f: (e, i, 0)),
            pl.BlockSpec((1, D, tf), lambda e, i, f: (e, 0, f)),
            pl.BlockSpec((1, D, tf), lambda e, i, f: (e, 0, f)),
            pl.BlockSpec((1, tf, D), lambda e, i, f: (e, f, 0)),
        ],
        out_specs=pl.BlockSpec((1, tm, D), lambda e, i, f: (e, i, 0)),
        out_shape=jax.ShapeDtypeStruct((E, C, D), jnp.float32),
        compiler_params=_compiler_params(("parallel", "parallel", "arbitrary")),
        name="moe_ffn",
    )(xe, gates, wg, wu, wd)


def _final_kernel(a_ref, b_ref, g_ref, o_ref):
    x = a_ref[...] + b_ref[...]
    o_ref[...] = x * _rms_scale(x) * g_ref[...]


def final_combine(x1, moe, g):
    N, D = x1.shape
    tm = min(ROW_TILE, N)
    row = pl.BlockSpec((tm, D), lambda i: (i, 0))
    return pl.pallas_call(
        _final_kernel,
        grid=(N // tm,),
        in_specs=[row, row, pl.BlockSpec((1, D), lambda i: (0, 0))],
        out_specs=row,
        out_shape=jax.ShapeDtypeStruct((N, D), jnp.float32),
        compiler_params=_compiler_params(("parallel",)),
        name="final_norm",
    )(x1, moe, g)


DFT_P = 128
SLAB_PAD = 8
X_PITCH = DFT_P + SLAB_PAD
A_PITCH = 2 * DFT_P + SLAB_PAD
HY_SLABS = HYENA_WIDTH // LANES
U_SLABS = 3 * HY_SLABS
SLOW_STAGE_UNROLL = 8
SPECTRAL_UNROLL = 4


def _dft_sizes(M):
    q = M // DFT_P
    ka = q // 2 + 1
    kp = -(-ka // 8) * 8
    return q, ka, kp


def _dft_constants(M):
    p = DFT_P
    q, ka, kp = _dft_sizes(M)
    k = np.arange(ka)

    def slow_fwd(n_ts):
        ang = 2.0 * np.pi * ((k[:, None] * np.arange(n_ts)[None, :]) % q) / q
        m = np.zeros((2 * kp, n_ts))
        m[:ka] = np.cos(ang)
        m[kp:kp + ka] = -np.sin(ang)
        return m

    ts = np.arange(q // 2)
    ang = 2.0 * np.pi * ((ts[:, None] * k[None, :]) % q) / q
    c = np.where((k == 0) | (k == q // 2), 1.0, 2.0)[None, :]
    slow_inv = np.zeros((q // 2, 2 * kp))
    slow_inv[:, :ka] = c * np.cos(ang) / M
    slow_inv[:, kp:kp + ka] = -c * np.sin(ang) / M

    kb = np.arange(p)
    tf = np.arange(p)
    idx = (tf[None, None, :] * (k[:, None, None] + q * kb[None, :, None])) % M
    wr = np.cos(2.0 * np.pi * idx / M)
    wi = -np.sin(2.0 * np.pi * idx / M)
    w2 = np.concatenate([np.concatenate([wr, -wi], axis=2), np.concatenate([wi, wr], axis=2)], axis=1)
    bf = lambda a: jnp.asarray(a, jnp.float32).astype(jnp.bfloat16)
    return {"fa_half": bf(slow_fwd(q // 2)), "fa_full": bf(slow_fwd(q)), "fi": bf(slow_inv),
            "w2": bf(w2), "v2": bf(np.transpose(w2, (0, 2, 1)))}


def _slow_stage_forward(xs_ref, ag_ref, fa_ref, n_ts, kp):
    fa = fa_ref[...]

    def body(tf, carry):
        xt = xs_ref[pl.ds(tf, n_ts, stride=X_PITCH), :].astype(jnp.bfloat16)
        a = jnp.dot(fa, xt, preferred_element_type=jnp.float32)
        ag_ref[pl.ds(tf, kp, stride=A_PITCH), :] = a[:kp]
        ag_ref[pl.ds(DFT_P + tf, kp, stride=A_PITCH), :] = a[kp:]
        return carry

    lax.fori_loop(0, DFT_P, body, 0, unroll=SLOW_STAGE_UNROLL)


def _slow_stage_inverse(ag_ref, ys_ref, fi_ref, n_ts, kp):
    fi = fi_ref[...]

    def body(tf, carry):
        gr = ag_ref[pl.ds(tf, kp, stride=A_PITCH), :]
        gi = ag_ref[pl.ds(DFT_P + tf, kp, stride=A_PITCH), :]
        g = jnp.concatenate([gr, gi], axis=0).astype(jnp.bfloat16)
        ys_ref[pl.ds(tf, n_ts, stride=X_PITCH), :] = jnp.dot(fi, g, preferred_element_type=jnp.float32)
        return carry

    lax.fori_loop(0, DFT_P, body, 0, unroll=SLOW_STAGE_UNROLL)


def _fast_stage(ag_ref, w2_ref, k):
    r0 = pl.multiple_of(k * A_PITCH, 8)
    a2 = ag_ref[pl.ds(r0, 2 * DFT_P), :].astype(jnp.bfloat16)
    return r0, jnp.dot(w2_ref[k], a2, preferred_element_type=jnp.float32)


def _spectral_product(ag_ref, w2_ref, v2_ref, kspec_ref, ka):
    p = DFT_P

    def block(k):
        r0, x2 = _fast_stage(ag_ref, w2_ref, k)
        kk = kspec_ref[0, 0, pl.ds(pl.multiple_of(k * 2 * p, 2 * p), 2 * p), :]
        xr, xi, kr, ki = x2[:p], x2[p:], kk[:p], kk[p:]
        y2 = jnp.concatenate([xr * kr - xi * ki, xr * ki + xi * kr], axis=0).astype(jnp.bfloat16)
        return r0, jnp.dot(v2_ref[k], y2, preferred_element_type=jnp.float32)

    def body(j, carry):
        done = [block(SPECTRAL_UNROLL * j + i) for i in range(SPECTRAL_UNROLL)]
        for r0, g2 in done:
            ag_ref[pl.ds(r0, 2 * p), :] = g2
        return carry

    lax.fori_loop(0, ka // SPECTRAL_UNROLL, body, 0)
    for k in range(ka - ka % SPECTRAL_UNROLL, ka):
        r0, g2 = block(k)
        ag_ref[pl.ds(r0, 2 * p), :] = g2


def _hyena_kernel(uv_ref, u1_ref, u2_ref, sw_ref, sb_ref, skip_ref, k0_ref, k1_ref, fa_ref, fi_ref, w2_ref,
                  v2_ref, o_ref, xs_ref, ys_ref, ag_ref, *, L):
    p = DFT_P
    c = pl.program_id(0)
    n_ts = L // p
    _, ka, kp = _dft_sizes(2 * L)
    row = lax.broadcasted_iota(jnp.int32, (p, LANES), 0)

    def short_conv_block(u_ref, group, ts):
        r0 = pl.multiple_of(ts * p, p)
        mid = u_ref[0, 0, pl.ds(r0, p), :]
        prev8 = u_ref[0, 0, pl.ds(pl.multiple_of(jnp.maximum(r0 - 8, 0), 8), 8), :]
        next8 = u_ref[0, 0, pl.ds(pl.multiple_of(jnp.minimum(r0 + p, L - 8), 8), 8), :]
        has_prev = (r0 > 0).astype(jnp.float32)
        has_next = (r0 + p < L).astype(jnp.float32)
        up = jnp.where(row == 0, prev8[7:8, :] * has_prev, pltpu.roll(mid, 1, 0))
        dn = jnp.where(row == p - 1, next8[0:1, :] * has_next, pltpu.roll(mid, p - 1, 0))
        ch = group * HY_SLABS + c
        w = lambda tap: sw_ref[pl.ds(tap * U_SLABS + ch, 1), :]
        return up * w(0) + mid * w(1) + dn * w(2) + sb_ref[pl.ds(ch, 1), :]

    def long_conv(kspec_ref):
        _slow_stage_forward(xs_ref, ag_ref, fa_ref, n_ts, kp)
        _spectral_product(ag_ref, w2_ref, v2_ref, kspec_ref, ka)
        _slow_stage_inverse(ag_ref, ys_ref, fi_ref, n_ts, kp)

    def blocks(fn):
        def body(ts, carry):
            fn(ts, pl.ds(pl.multiple_of(ts * X_PITCH, 8), p))
            return carry
        lax.fori_loop(0, n_ts, body, 0)

    def stage_v(ts, rows):
        xs_ref[rows, :] = short_conv_block(uv_ref, 0, ts)

    def stage_z(ts, rows):
        v = xs_ref[rows, :]
        y1 = ys_ref[rows, :] + skip_ref[pl.ds(c, 1), :] * v
        xs_ref[rows, :] = short_conv_block(u1_ref, 1, ts) * y1

    def stage_out(ts, rows):
        z = xs_ref[rows, :]
        y2 = ys_ref[rows, :] + skip_ref[pl.ds(HY_SLABS + c, 1), :] * z
        o_ref[0, 0, pl.ds(pl.multiple_of(ts * p, p), p), :] = short_conv_block(u2_ref, 2, ts) * y2

    blocks(stage_v)
    long_conv(k0_ref)
    blocks(stage_z)
    long_conv(k1_ref)
    blocks(stage_out)


def hyena_mix(u, sw, sb, skip, kspec, consts):
    B, n_slabs, L, _ = u.shape
    q, ka, kp = _dft_sizes(2 * L)
    n_ts = L // DFT_P
    u_spec = lambda g: pl.BlockSpec((1, 1, L, LANES), lambda c, b: (b, g * HY_SLABS + c, 0, 0))
    k_spec = lambda o: pl.BlockSpec((1, 1, ka * 2 * DFT_P, LANES), lambda c, b: (o, c, 0, 0),
                                    pipeline_mode=pl.Buffered(1))
    const = lambda a: pl.BlockSpec(a.shape, lambda c, b: (0,) * a.ndim, pipeline_mode=pl.Buffered(1))
    small = lambda a: pl.BlockSpec(a.shape, lambda c, b: (0,) * a.ndim)
    fa, fi, w2, v2 = consts["fa_half"], consts["fi"], consts["w2"], consts["v2"]
    return pl.pallas_call(
        partial(_hyena_kernel, L=L),
        grid=(HY_SLABS, B),
        in_specs=[u_spec(0), u_spec(1), u_spec(2), small(sw), small(sb), small(skip), k_spec(0), k_spec(1),
                  small(fa), small(fi), const(w2), const(v2)],
        out_specs=pl.BlockSpec((1, 1, L, LANES), lambda c, b: (b, c, 0, 0)),
        out_shape=jax.ShapeDtypeStruct((B, HY_SLABS, L, LANES), jnp.float32),
        scratch_shapes=[pltpu.VMEM((n_ts * X_PITCH, LANES), jnp.float32),
                        pltpu.VMEM((n_ts * X_PITCH, LANES), jnp.float32),
                        pltpu.VMEM((kp * A_PITCH, LANES), jnp.float32)],
        compiler_params=_compiler_params(("arbitrary", "arbitrary")),
        name="hyena_mix",
    )(u, u, u, sw, sb, skip, kspec, kspec, fa, fi, w2, v2)


def _filter_spectrum_kernel(kern_ref, fa_ref, w2_ref, o_ref, xs_ref, ag_ref, *, M):
    p = DFT_P
    q, ka, kp = _dft_sizes(M)

    def load(ts, carry):
        xs_ref[pl.ds(pl.multiple_of(ts * X_PITCH, 8), p), :] = kern_ref[0, 0, pl.ds(pl.multiple_of(ts * p, p), p), :]
        return carry

    lax.fori_loop(0, q, load, 0)
    _slow_stage_forward(xs_ref, ag_ref, fa_ref, q, kp)

    def body(k, carry):
        _, x2 = _fast_stage(ag_ref, w2_ref, k)
        o_ref[0, 0, pl.ds(pl.multiple_of(k * 2 * p, 2 * p), 2 * p), :] = x2
        return carry

    lax.fori_loop(0, ka, body, 0)


def filter_spectrum(kern, consts):
    n_o, n_c, M, _ = kern.shape
    q, ka, kp = _dft_sizes(M)
    fa, w2 = consts["fa_full"], consts["w2"]
    full = lambda a: pl.BlockSpec(a.shape, lambda o, c: (0,) * a.ndim)
    return pl.pallas_call(
        partial(_filter_spectrum_kernel, M=M),
        grid=(n_o, n_c),
        in_specs=[pl.BlockSpec((1, 1, M, LANES), lambda o, c: (o, c, 0, 0)), full(fa), full(w2)],
        out_specs=pl.BlockSpec((1, 1, ka * 2 * DFT_P, LANES), lambda o, c: (o, c, 0, 0)),
        out_shape=jax.ShapeDtypeStruct((n_o, n_c, ka * 2 * DFT_P, LANES), jnp.float32),
        scratch_shapes=[pltpu.VMEM((q * X_PITCH, LANES), jnp.float32),
                        pltpu.VMEM((kp * A_PITCH, LANES), jnp.float32)],
        compiler_params=_compiler_params(("arbitrary", "arbitrary")),
        name="filter_spectrum",
    )(kern, fa, w2)


FILTER_ROW_TILE = 512


def _filter_kernel(bands_ref, w1t_ref, w1c_ref, w1s_ref, b1_ref, w2_ref, b2_ref, w3_ref, b3_ref, w4_ref, freq_ref,
                   absd_ref, o_ref, *, L):
    tr = o_ref.shape[2]
    hp = lax.Precision.HIGHEST
    rho = pl.program_id(0) * tr + lax.broadcasted_iota(jnp.int32, (tr, 1), 0)
    tau = jnp.where(rho < L, rho, 2 * L - rho).astype(jnp.float32)
    t = tau * (1.0 / (L - 1))
    ang = (2.0 * math.pi * tau / L) * bands_ref[...]
    dot = lambda a, b: jnp.dot(a, b, preferred_element_type=jnp.float32, precision=hp)
    fr = freq_ref[...]
    pre = t * w1t_ref[...] + dot(jnp.cos(ang), w1c_ref[...]) - dot(jnp.sin(ang), w1s_ref[...]) + b1_ref[...]
    h = jnp.sin(fr * pre)
    h = jnp.sin(fr * (dot(h, w2_ref[...]) + b2_ref[...]))
    h = jnp.sin(fr * (dot(h, w3_ref[...]) + b3_ref[...]))
    hh = dot(h, w4_ref[...])
    decay = jnp.exp(-t * absd_ref[...])
    fwd = rho < L
    nonzero = rho != L
    w = HYENA_WIDTH
    for o in range(HYENA_ORDER):
        for c in range(HY_SLABS):
            f = hh[:, (2 * o) * w + c * LANES:(2 * o) * w + (c + 1) * LANES]
            b = hh[:, (2 * o + 1) * w + c * LANES:(2 * o + 1) * w + (c + 1) * LANES]
            val = jnp.where(fwd, f, b) * decay[:, c * LANES:(c + 1) * LANES]
            o_ref[o, c] = jnp.where(nonzero, val, 0.0)


def hyena_kernels(L, w1, b1, w2, b2, w3, b3, w4, freq):
    M = 2 * L
    tr = min(FILTER_ROW_TILE, M)
    bands = np.zeros((1, LANES), np.float32)
    bands[0, :FILTER_BANDS] = np.linspace(1e-4, FILTER_BANDS - 1, FILTER_BANDS, dtype=np.float32)
    pad_rows = lambda a: jnp.zeros((LANES, a.shape[1]), jnp.float32).at[:a.shape[0]].set(a)
    min_decay = math.log(DECAY_TARGET) / FAST_DECAY_PCT
    max_decay = math.log(DECAY_TARGET) / SLOW_DECAY_PCT
    absd = np.abs(np.linspace(min_decay, max_decay, HYENA_WIDTH, dtype=np.float32))[None, :]
    args = [jnp.asarray(bands), w1[0:1], pad_rows(w1[1:1 + FILTER_BANDS]), pad_rows(w1[1 + FILTER_BANDS:]),
            b1[None, :], w2, b2[None, :], w3, b3[None, :], w4, freq[None, :], jnp.asarray(absd)]
    full = lambda a: pl.BlockSpec(a.shape, lambda i: (0,) * a.ndim)
    return pl.pallas_call(
        partial(_filter_kernel, L=L),
        grid=(M // tr,),
        in_specs=[full(a) for a in args],
        out_specs=pl.BlockSpec((HYENA_ORDER, HY_SLABS, tr, LANES), lambda i: (0, 0, i, 0)),
        out_shape=jax.ShapeDtypeStruct((HYENA_ORDER, HY_SLABS, M, LANES), jnp.float32),
        compiler_params=_compiler_params(("parallel",)),
        name="hyena_filter",
    )(*args)


def hyena_operator(u, p):
    L = u.shape[2]
    consts = _dft_constants(2 * L)
    kern = hyena_kernels(L, p["filt_w1"], p["filt_b1"], p["filt_w2"], p["filt_b2"], p["filt_w3"], p["filt_b3"],
                         p["filt_w4"], p["filt_freq"])
    kspec = filter_spectrum(kern, consts)
    sw = p["short_w"].reshape(SHORT_CONV * U_SLABS, LANES)
    sb = p["short_b"].reshape(U_SLABS, LANES)
    skip = p["hyena_skip"].reshape(HYENA_ORDER * HY_SLABS, LANES)
    return hyena_mix(u, sw, sb, skip, kspec, consts)


def expert_choice_ffn(h, affinity, wg, wu, wd):
    N, D = h.shape
    cap = EC_CAPACITY * N // N_EXPERTS
    gates, idx = lax.top_k(affinity.T, cap)
    ye = expert_ffn(h[idx], gates[..., None], wg, wu, wd)
    return jnp.zeros((N, D), jnp.float32).at[idx.reshape(-1)].add(ye.reshape(-1, D))


def encoder_trunk(x, p):
    B, L, D = x.shape
    qkv, u = in_proj(x, p["norm_mix_g"], p["w_in"])
    attn_n = dilated_attention(qkv, p["attn_out_g"])
    hy = hyena_operator(u, p)
    x1, h, affinity = out_proj(x, attn_n, hy, p["hyena_out_g"], p["w_out"], p["norm_ffn_g"], p["w_router"])
    moe = expert_choice_ffn(h, affinity, p["w_gate"], p["w_up"], p["w_down"])
    return final_combine(x1, moe, p["final_g"]).reshape(B, L, D)


def kernel(x_prompt, x_sample, norm_mix_g, w_in, short_w, short_b, filt_w1, filt_b1, filt_w2, filt_b2, filt_w3, filt_b3, filt_w4, filt_freq, hyena_skip, attn_out_g, hyena_out_g, w_out, norm_ffn_g, w_router, w_gate, w_up, w_down, final_g):
    bf16 = jnp.bfloat16
    p = {
        "norm_mix_g": norm_mix_g, "w_in": w_in[0].astype(bf16),
        "short_w": short_w[0], "short_b": short_b[0],
        "filt_w1": filt_w1[0], "filt_b1": filt_b1[0], "filt_w2": filt_w2[0], "filt_b2": filt_b2[0],
        "filt_w3": filt_w3[0], "filt_b3": filt_b3[0], "filt_w4": filt_w4[0], "filt_freq": filt_freq[0],
        "hyena_skip": hyena_skip[0], "attn_out_g": attn_out_g, "hyena_out_g": hyena_out_g,
        "w_out": w_out[0].astype(bf16), "norm_ffn_g": norm_ffn_g, "w_router": w_router[0],
        "w_gate": w_gate[0].astype(bf16), "w_up": w_up[0].astype(bf16), "w_down": w_down[0].astype(bf16),
        "final_g": final_g.reshape(1, D_MODEL),
    }
    return (encoder_trunk(x_prompt, p), encoder_trunk(x_sample, p))
```

```python
import math
from functools import partial

import jax
import jax.numpy as jnp
import numpy as np
from jax import lax
from jax.experimental import pallas as pl
from jax.experimental.pallas import tpu as pltpu

D_MODEL = 1024
HEAD_DIM = 64
N_ATTN_HEADS = 8
ATTN_WIDTH = N_ATTN_HEADS * HEAD_DIM
HYENA_WIDTH = D_MODEL - ATTN_WIDTH
HYENA_ORDER = 2
SHORT_CONV = 3
FILTER_EMB = 33
FILTER_BANDS = (FILTER_EMB - 1) // 2
FAST_DECAY_PCT = 0.3
SLOW_DECAY_PCT = 1.5
DECAY_TARGET = 1e-2
DILATED_PATTERNS = ((128, 1), (512, 4), (2048, 16))
N_EXPERTS = 16
EC_CAPACITY = 2
D_EXPERT = 2816
NORM_EPS = 1e-6
MASK_VALUE = -1e30

LANES = 128
VMEM_LIMIT_BYTES = 56 * 1024 * 1024

ROW_TILE = 512
FFN_TOKEN_TILE = 1024
FFN_HIDDEN_TILE = 1408
ATTN_Q_TILE = 128
ATTN_TILES_PER_TRIP = 4


def _compiler_params(semantics):
    return pltpu.CompilerParams(dimension_semantics=semantics, vmem_limit_bytes=VMEM_LIMIT_BYTES)


def _rms_scale(x):
    return lax.rsqrt(jnp.mean(x * x, axis=-1, keepdims=True) + NORM_EPS)


def _in_proj_kernel(x_ref, g_ref, w_ref, qkv_ref, u_ref):
    x = x_ref[...]
    h = (x * _rms_scale(x) * g_ref[...]).astype(jnp.bfloat16)
    a = ATTN_WIDTH
    qkv = jnp.dot(h, w_ref[:, :3 * a], preferred_element_type=jnp.float32)
    for s in range(qkv_ref.shape[1]):
        cols = qkv[:, s * LANES:(s + 1) * LANES]
        qkv_ref[0, s] = cols * (HEAD_DIM ** -0.5) if s < a // LANES else cols
    u = jnp.dot(h, w_ref[:, 3 * a:], preferred_element_type=jnp.float32)
    for s in range(u_ref.shape[1]):
        u_ref[0, s] = u[:, s * LANES:(s + 1) * LANES]


def in_proj(x, g, w):
    B, L, D = x.shape
    N = B * L
    tm = min(ROW_TILE, L)
    assert L % tm == 0
    nt = L // tm
    a3, h3 = 3 * ATTN_WIDTH, 3 * HYENA_WIDTH
    slabs = lambda width: pl.BlockSpec((1, width // LANES, tm, LANES), lambda i: (i // nt, 0, i % nt, 0))
    return pl.pallas_call(
        _in_proj_kernel,
        grid=(N // tm,),
        in_specs=[
            pl.BlockSpec((tm, D), lambda i: (i, 0)),
            pl.BlockSpec((1, D), lambda i: (0, 0)),
            pl.BlockSpec((D, a3 + h3), lambda i: (0, 0)),
        ],
        out_specs=[slabs(a3), slabs(h3)],
        out_shape=[jax.ShapeDtypeStruct((B, a3 // LANES, L, LANES), jnp.float32),
                   jax.ShapeDtypeStruct((B, h3 // LANES, L, LANES), jnp.float32)],
        compiler_params=_compiler_params(("parallel",)),
        name="in_proj",
    )(x.reshape(N, D), g, w)


def _alibi_slope(h):
    return 2.0 ** (-8.0 * (h + 1) / N_ATTN_HEADS)


def _attn_tiling(L, window, dilation):
    radius = window // (2 * dilation)
    n = L // dilation
    tq = min(ATTN_Q_TILE, n)
    tk = min(tq + 2 * radius, n)
    return radius, n, tq, tk


def _attn_kernel(q_ref, k_ref, v_ref, o_ref, lse_ref, *bias_refs, L, hp_axis):
    hp = pl.program_id(hp_axis)

    def pick(values):
        out = jnp.float32(values[-1])
        for j in range(len(values) - 2, -1, -1):
            out = jnp.where(hp == j, jnp.float32(values[j]), out)
        return out

    n_pairs = ATTN_WIDTH // LANES
    slope_a = pick([_alibi_slope(2 * j) for j in range(n_pairs)])
    slope_b = pick([_alibi_slope(2 * j + 1) for j in range(n_pairs)])

    for branch, (window, dilation) in enumerate(DILATED_PATTERNS):
        radius, n, tq, tk = _attn_tiling(L, window, dilation)
        tiles = n // tq
        first = branch == 0

        lane = lax.broadcasted_iota(jnp.int32, (tq, LANES), 1)
        low = lane < HEAD_DIM
        row2 = lax.broadcasted_iota(jnp.int32, (2 * tq, tk), 0)
        col2 = lax.broadcasted_iota(jnp.int32, (2 * tq, tk), 1)
        base_rel = col2 - jnp.where(row2 >= tq, row2 - tq, row2)
        second_col = lax.broadcasted_iota(jnp.int32, (2 * tq, 1), 0) >= tq
        slope = jnp.where(second_col, slope_b, slope_a) * float(dilation)

        offsets = {q0 - min(max(q0 - radius, 0), n - tk) for q0 in range(0, n, tq)}
        assert offsets <= {0, radius, 2 * radius}
        bias_ref = bias_refs[branch]
        for case in range(3):
            absrel = jnp.abs(base_rel - case * radius)
            bias_ref[case] = jnp.where(absrel <= radius, -(slope * absrel.astype(jnp.float32)), MASK_VALUE)

        def tile(it, carry, dilation=dilation, radius=radius, n=n, tq=tq, tk=tk, tiles=tiles, first=first,
                 low=low, lane=lane, bias_ref=bias_ref):
            r = it // tiles
            q0 = (it % tiles) * tq
            ks = jnp.clip(q0 - radius, 0, n - tk)
            q_rows = pl.ds(r + dilation * q0, tq, stride=dilation)
            k_rows = pl.ds(r + dilation * ks, tk, stride=dilation)
            q = q_ref[0, 0, q_rows, :].astype(jnp.bfloat16)
            kw = k_ref[0, 0, k_rows, :].astype(jnp.bfloat16)
            vw = v_ref[0, 0, k_rows, :].astype(jnp.bfloat16)
            zero = jnp.zeros_like(q)
            q2 = jnp.concatenate([jnp.where(low, q, zero), jnp.where(low, zero, q)], axis=0)
            s = lax.dot_general(q2, kw, (((1,), (1,)), ((), ())), preferred_element_type=jnp.float32)
            s = s + bias_ref[(q0 - ks) // radius]
            m = jnp.max(s, axis=-1, keepdims=True)
            p = jnp.exp(s - m)
            l = jnp.sum(p, axis=-1, keepdims=True)
            o2 = jnp.dot(p.astype(jnp.bfloat16), vw, preferred_element_type=jnp.float32) / l
            lse2 = m + jnp.log(l)
            o = jnp.where(low, o2[:tq], o2[tq:])
            lse = jnp.where(lane == 0, lse2[:tq], jnp.where(lane == 1, lse2[tq:], 0.0))
            if not first:
                prev = lse_ref[q_rows, :]
                mx = jnp.maximum(prev, lse)
                new = mx + jnp.log(jnp.exp(prev - mx) + jnp.exp(lse - mx))
                w_prev = jnp.exp(prev - new)
                w_cur = jnp.exp(lse - new)
                o = (o_ref[0, 0, q_rows, :] * jnp.where(low, w_prev[:, 0:1], w_prev[:, 1:2])
                     + o * jnp.where(low, w_cur[:, 0:1], w_cur[:, 1:2]))
                lse = new
            return q_rows, o, lse

        def tiles_body(j, carry, tile=tile):
            done = [tile(ATTN_TILES_PER_TRIP * j + i, carry) for i in range(ATTN_TILES_PER_TRIP)]
            for q_rows, o, lse in done:
                o_ref[0, 0, q_rows, :] = o
                lse_ref[q_rows, :] = lse
            return carry

        assert (dilation * tiles) % ATTN_TILES_PER_TRIP == 0
        lax.fori_loop(0, dilation * tiles // ATTN_TILES_PER_TRIP, tiles_body, 0)


def dilated_attention(qkv):
    B, _, L, _ = qkv.shape
    n_pairs = ATTN_WIDTH // LANES
    bias_scratch = []
    for window, dilation in DILATED_PATTERNS:
        radius, n, tq, tk = _attn_tiling(L, window, dilation)
        assert L % dilation == 0 and n % tq == 0
        bias_scratch.append(pltpu.VMEM((3, 2 * tq, tk), jnp.float32))
    spec = lambda g: pl.BlockSpec((1, 1, L, LANES), lambda b, hp: (b, g * n_pairs + hp, 0, 0))
    return pl.pallas_call(
        partial(_attn_kernel, L=L, hp_axis=1),
        grid=(B, n_pairs),
        in_specs=[spec(0), spec(1), spec(2)],
        out_specs=pl.BlockSpec((1, 1, L, LANES), lambda b, hp: (b, hp, 0, 0)),
        out_shape=jax.ShapeDtypeStruct((B, n_pairs, L, LANES), jnp.float32),
        scratch_shapes=[pltpu.VMEM((L, LANES), jnp.float32)] + bias_scratch,
        compiler_params=_compiler_params(("parallel", "arbitrary")),
        name="dilated_attn",
    )(qkv, qkv, qkv)


def _out_proj_kernel(x_ref, attn_ref, hy_ref, ag_ref, hg_ref, w_ref, ng_ref, wr_ref, x1_ref, h_ref, aff_ref):
    a = ATTN_WIDTH

    def group_norm(ref, g_ref):
        v = jnp.concatenate([ref[0, s] for s in range(ref.shape[1])], axis=-1)
        return (v * _rms_scale(v) * g_ref[...]).astype(jnp.bfloat16)

    y = jnp.dot(group_norm(attn_ref, ag_ref), w_ref[:a, :], preferred_element_type=jnp.float32)
    y = y + jnp.dot(group_norm(hy_ref, hg_ref), w_ref[a:, :], preferred_element_type=jnp.float32)
    x1 = x_ref[...] + y
    x1_ref[...] = x1
    h = x1 * _rms_scale(x1) * ng_ref[...]
    h_ref[...] = h.astype(jnp.bfloat16)
    logits = jnp.dot(h, wr_ref[...], preferred_element_type=jnp.float32, precision=lax.Precision.HIGHEST)
    e = jnp.exp(logits - jnp.max(logits, axis=-1, keepdims=True))
    aff_ref[...] = e / jnp.sum(e, axis=-1, keepdims=True)


def out_proj(x, attn, hy, attn_g, hy_g, w, ffn_g, w_router):
    B, L, D = x.shape
    N = B * L
    tm = min(ROW_TILE, L)
    assert L % tm == 0
    nt = L // tm
    E = w_router.shape[-1]
    slabs = lambda width: pl.BlockSpec((1, width // LANES, tm, LANES), lambda i: (i // nt, 0, i % nt, 0))
    row = lambda width: pl.BlockSpec((tm, width), lambda i: (i, 0))
    full = lambda r, c: pl.BlockSpec((r, c), lambda i: (0, 0))
    return pl.pallas_call(
        _out_proj_kernel,
        grid=(N // tm,),
        in_specs=[row(D), slabs(ATTN_WIDTH), slabs(HYENA_WIDTH), full(1, ATTN_WIDTH), full(1, HYENA_WIDTH),
                  full(D, D), full(1, D), full(D, E)],
        out_specs=[row(D), row(D), row(E)],
        out_shape=[jax.ShapeDtypeStruct((N, D), jnp.float32), jax.ShapeDtypeStruct((N, D), jnp.bfloat16),
                   jax.ShapeDtypeStruct((N, E), jnp.float32)],
        compiler_params=_compiler_params(("parallel",)),
        name="out_proj",
    )(x.reshape(N, D), attn, hy, attn_g, hy_g, w, ffn_g, w_router)


def _ffn_kernel(x_ref, gate_ref, wg_ref, wu_ref, wd_ref, o_ref):
    f = pl.program_id(2)
    x = x_ref[0]
    a = jnp.dot(x, wg_ref[0], preferred_element_type=jnp.float32)
    b = jnp.dot(x, wu_ref[0], preferred_element_type=jnp.float32)
    h = (a * jax.nn.sigmoid(a) * b).astype(jnp.bfloat16)
    y = jnp.dot(h, wd_ref[0], preferred_element_type=jnp.float32)

    @pl.when(f == 0)
    def _():
        o_ref[0] = y

    @pl.when(f > 0)
    def _():
        o_ref[0] += y

    @pl.when(f == pl.num_programs(2) - 1)
    def _():
        o_ref[0] *= gate_ref[0]


def expert_ffn(xe, gates, wg, wu, wd):
    E, C, D = xe.shape
    F = wg.shape[-1]
    tm = min(FFN_TOKEN_TILE, C)
    tf = min(FFN_HIDDEN_TILE, F)
    assert C % tm == 0 and F % tf == 0
    return pl.pallas_call(
        _ffn_kernel,
        grid=(E, C // tm, F // tf),
        in_specs=[
            pl.BlockSpec((1, tm, D), lambda e, i, f: (e, i, 0)),
            pl.BlockSpec((1, tm, 1), lambda e, i, f: (e, i, 0)),
            pl.BlockSpec((1, D, tf), lambda e, i, f: (e, 0, f)),
            pl.BlockSpec((1, D, tf), lambda e, i, f: (e, 0, f)),
            pl.BlockSpec((1, tf, D), lambda e, i, f: (e, f, 0)),
        ],
        out_specs=pl.BlockSpec((1, tm, D), lambda e, i, f: (e, i, 0)),
        out_shape=jax.ShapeDtypeStruct((E, C, D), jnp.float32),
        compiler_params=_compiler_params(("parallel", "parallel", "arbitrary")),
        name="moe_ffn",
    )(xe, gates, wg, wu, wd)


def _final_kernel(x_ref, g_ref, o_ref):
    x = x_ref[...]
    o_ref[...] = x * _rms_scale(x) * g_ref[...]


def final_norm(x, g):
    N, D = x.shape
    tm = min(ROW_TILE, N)
    row = pl.BlockSpec((tm, D), lambda i: (i, 0))
    return pl.pallas_call(
        _final_kernel,
        grid=(N // tm,),
        in_specs=[row, pl.BlockSpec((1, D), lambda i: (0, 0))],
        out_specs=row,
        out_shape=jax.ShapeDtypeStruct((N, D), jnp.float32),
        compiler_params=_compiler_params(("parallel",)),
        name="final_norm",
    )(x, g)


DFT_P = 128
SLAB_PAD = 8
X_PITCH = DFT_P + SLAB_PAD
A_PITCH = 2 * DFT_P + SLAB_PAD
HY_SLABS = HYENA_WIDTH // LANES
U_SLABS = 3 * HY_SLABS
SLOW_STAGE_UNROLL = 8
SPECTRAL_UNROLL = 4


def _dft_sizes(M):
    q = M // DFT_P
    ka = q // 2 + 1
    kp = -(-ka // 8) * 8
    return q, ka, kp


def _dft_constants(M):
    p = DFT_P
    q, ka, kp = _dft_sizes(M)
    k = np.arange(ka)

    def slow_fwd(n_ts):
        ang = 2.0 * np.pi * ((k[:, None] * np.arange(n_ts)[None, :]) % q) / q
        m = np.zeros((2 * kp, n_ts))
        m[:ka] = np.cos(ang)
        m[kp:kp + ka] = -np.sin(ang)
        return m

    ts = np.arange(q // 2)
    ang = 2.0 * np.pi * ((ts[:, None] * k[None, :]) % q) / q
    c = np.where((k == 0) | (k == q // 2), 1.0, 2.0)[None, :]
    slow_inv = np.zeros((q // 2, 2 * kp))
    slow_inv[:, :ka] = c * np.cos(ang) / M
    slow_inv[:, kp:kp + ka] = -c * np.sin(ang) / M

    kb = np.arange(p)
    tf = np.arange(p)
    idx = (tf[None, None, :] * (k[:, None, None] + q * kb[None, :, None])) % M
    wr = np.cos(2.0 * np.pi * idx / M)
    wi = -np.sin(2.0 * np.pi * idx / M)
    w2 = np.concatenate([np.concatenate([wr, -wi], axis=2), np.concatenate([wi, wr], axis=2)], axis=1)
    bf = lambda a: jnp.asarray(a, jnp.float32).astype(jnp.bfloat16)
    return {"fa_half": bf(slow_fwd(q // 2)), "fa_full": bf(slow_fwd(q)), "fi": bf(slow_inv),
            "w2": bf(w2), "v2": bf(np.transpose(w2, (0, 2, 1)))}


def _slow_stage_forward(xs_ref, ag_ref, fa_ref, n_ts, kp):
    fa = fa_ref[...]

    def body(tf, carry):
        xt = xs_ref[pl.ds(tf, n_ts, stride=X_PITCH), :].astype(jnp.bfloat16)
        a = jnp.dot(fa, xt, preferred_element_type=jnp.float32)
        ag_ref[pl.ds(tf, kp, stride=A_PITCH), :] = a[:kp]
        ag_ref[pl.ds(DFT_P + tf, kp, stride=A_PITCH), :] = a[kp:]
        return carry

    lax.fori_loop(0, DFT_P, body, 0, unroll=SLOW_STAGE_UNROLL)


def _slow_stage_inverse(ag_ref, ys_ref, fi_ref, n_ts, kp):
    fi = fi_ref[...]

    def body(tf, carry):
        gr = ag_ref[pl.ds(tf, kp, stride=A_PITCH), :]
        gi = ag_ref[pl.ds(DFT_P + tf, kp, stride=A_PITCH), :]
        g = jnp.concatenate([gr, gi], axis=0).astype(jnp.bfloat16)
        ys_ref[pl.ds(tf, n_ts, stride=X_PITCH), :] = jnp.dot(fi, g, preferred_element_type=jnp.float32)
        return carry

    lax.fori_loop(0, DFT_P, body, 0, unroll=SLOW_STAGE_UNROLL)


def _fast_stage(ag_ref, w2_ref, k):
    r0 = pl.multiple_of(k * A_PITCH, 8)
    a2 = ag_ref[pl.ds(r0, 2 * DFT_P), :].astype(jnp.bfloat16)
    return r0, jnp.dot(w2_ref[k], a2, preferred_element_type=jnp.float32)


def _spectral_product(ag_ref, w2_ref, v2_ref, kspec_ref, ka):
    p = DFT_P

    def block(k):
        r0, x2 = _fast_stage(ag_ref, w2_ref, k)
        kk = kspec_ref[0, 0, pl.ds(pl.multiple_of(k * 2 * p, 2 * p), 2 * p), :]
        xr, xi, kr, ki = x2[:p], x2[p:], kk[:p], kk[p:]
        y2 = jnp.concatenate([xr * kr - xi * ki, xr * ki + xi * kr], axis=0).astype(jnp.bfloat16)
        return r0, jnp.dot(v2_ref[k], y2, preferred_element_type=jnp.float32)

    def body(j, carry):
        done = [block(SPECTRAL_UNROLL * j + i) for i in range(SPECTRAL_UNROLL)]
        for r0, g2 in done:
            ag_ref[pl.ds(r0, 2 * p), :] = g2
        return carry

    lax.fori_loop(0, ka // SPECTRAL_UNROLL, body, 0)
    for k in range(ka - ka % SPECTRAL_UNROLL, ka):
        r0, g2 = block(k)
        ag_ref[pl.ds(r0, 2 * p), :] = g2


def _hyena_kernel(uv_ref, u1_ref, u2_ref, sw_ref, sb_ref, skip_ref, k0_ref, k1_ref, fa_ref, fi_ref, w2_ref,
                  v2_ref, o_ref, xs_ref, ys_ref, ag_ref, *, L):
    p = DFT_P
    c = pl.program_id(0)
    n_ts = L // p
    _, ka, kp = _dft_sizes(2 * L)
    row = lax.broadcasted_iota(jnp.int32, (p, LANES), 0)

    def short_conv_block(u_ref, group, ts):
        r0 = pl.multiple_of(ts * p, p)
        mid = u_ref[0, 0, pl.ds(r0, p), :]
        prev8 = u_ref[0, 0, pl.ds(pl.multiple_of(jnp.maximum(r0 - 8, 0), 8), 8), :]
        next8 = u_ref[0, 0, pl.ds(pl.multiple_of(jnp.minimum(r0 + p, L - 8), 8), 8), :]
        has_prev = (r0 > 0).astype(jnp.float32)
        has_next = (r0 + p < L).astype(jnp.float32)
        up = jnp.where(row == 0, prev8[7:8, :] * has_prev, pltpu.roll(mid, 1, 0))
        dn = jnp.where(row == p - 1, next8[0:1, :] * has_next, pltpu.roll(mid, p - 1, 0))
        ch = group * HY_SLABS + c
        w = lambda tap: sw_ref[pl.ds(tap * U_SLABS + ch, 1), :]
        return up * w(0) + mid * w(1) + dn * w(2) + sb_ref[pl.ds(ch, 1), :]

    def long_conv(kspec_ref):
        _slow_stage_forward(xs_ref, ag_ref, fa_ref, n_ts, kp)
        _spectral_product(ag_ref, w2_ref, v2_ref, kspec_ref, ka)
        _slow_stage_inverse(ag_ref, ys_ref, fi_ref, n_ts, kp)

    def blocks(fn):
        def body(ts, carry):
            fn(ts, pl.ds(pl.multiple_of(ts * X_PITCH, 8), p))
            return carry
        lax.fori_loop(0, n_ts, body, 0)

    def stage_v(ts, rows):
        xs_ref[rows, :] = short_conv_block(uv_ref, 0, ts)

    def stage_z(ts, rows):
        v = xs_ref[rows, :]
        y1 = ys_ref[rows, :] + skip_ref[pl.ds(c, 1), :] * v
        xs_ref[rows, :] = short_conv_block(u1_ref, 1, ts) * y1

    def stage_out(ts, rows):
        z = xs_ref[rows, :]
        y2 = ys_ref[rows, :] + skip_ref[pl.ds(HY_SLABS + c, 1), :] * z
        o_ref[0, 0, pl.ds(pl.multiple_of(ts * p, p), p), :] = short_conv_block(u2_ref, 2, ts) * y2

    blocks(stage_v)
    long_conv(k0_ref)
    blocks(stage_z)
    long_conv(k1_ref)
    blocks(stage_out)


def hyena_mix(u, sw, sb, skip, kspec, consts):
    B, n_slabs, L, _ = u.shape
    q, ka, kp = _dft_sizes(2 * L)
    n_ts = L // DFT_P
    u_spec = lambda g: pl.BlockSpec((1, 1, L, LANES), lambda c, b: (b, g * HY_SLABS + c, 0, 0))
    k_spec = lambda o: pl.BlockSpec((1, 1, ka * 2 * DFT_P, LANES), lambda c, b: (o, c, 0, 0),
                                    pipeline_mode=pl.Buffered(1))
    const = lambda a: pl.BlockSpec(a.shape, lambda c, b: (0,) * a.ndim, pipeline_mode=pl.Buffered(1))
    small = lambda a: pl.BlockSpec(a.shape, lambda c, b: (0,) * a.ndim)
    fa, fi, w2, v2 = consts["fa_half"], consts["fi"], consts["w2"], consts["v2"]
    return pl.pallas_call(
        partial(_hyena_kernel, L=L),
        grid=(HY_SLABS, B),
        in_specs=[u_spec(0), u_spec(1), u_spec(2), small(sw), small(sb), small(skip), k_spec(0), k_spec(1),
                  small(fa), small(fi), const(w2), const(v2)],
        out_specs=pl.BlockSpec((1, 1, L, LANES), lambda c, b: (b, c, 0, 0)),
        out_shape=jax.ShapeDtypeStruct((B, HY_SLABS, L, LANES), jnp.float32),
        scratch_shapes=[pltpu.VMEM((n_ts * X_PITCH, LANES), jnp.float32),
                        pltpu.VMEM((n_ts * X_PITCH, LANES), jnp.float32),
                        pltpu.VMEM((kp * A_PITCH, LANES), jnp.float32)],
        compiler_params=_compiler_params(("arbitrary", "arbitrary")),
        name="hyena_mix",
    )(u, u, u, sw, sb, skip, kspec, kspec, fa, fi, w2, v2)


def _filter_spectrum_kernel(kern_ref, fa_ref, w2_ref, o_ref, xs_ref, ag_ref, *, M):
    p = DFT_P
    q, ka, kp = _dft_sizes(M)

    def load(ts, carry):
        xs_ref[pl.ds(pl.multiple_of(ts * X_PITCH, 8), p), :] = kern_ref[0, 0, pl.ds(pl.multiple_of(ts * p, p), p), :]
        return carry

    lax.fori_loop(0, q, load, 0)
    _slow_stage_forward(xs_ref, ag_ref, fa_ref, q, kp)

    def body(k, carry):
        _, x2 = _fast_stage(ag_ref, w2_ref, k)
        o_ref[0, 0, pl.ds(pl.multiple_of(k * 2 * p, 2 * p), 2 * p), :] = x2
        return carry

    lax.fori_loop(0, ka, body, 0)


def filter_spectrum(kern, consts):
    n_o, n_c, M, _ = kern.shape
    q, ka, kp = _dft_sizes(M)
    fa, w2 = consts["fa_full"], consts["w2"]
    full = lambda a: pl.BlockSpec(a.shape, lambda o, c: (0,) * a.ndim)
    return pl.pallas_call(
        partial(_filter_spectrum_kernel, M=M),
        grid=(n_o, n_c),
        in_specs=[pl.BlockSpec((1, 1, M, LANES), lambda o, c: (o, c, 0, 0)), full(fa), full(w2)],
        out_specs=pl.BlockSpec((1, 1, ka * 2 * DFT_P, LANES), lambda o, c: (o, c, 0, 0)),
        out_shape=jax.ShapeDtypeStruct((n_o, n_c, ka * 2 * DFT_P, LANES), jnp.float32),
        scratch_shapes=[pltpu.VMEM((q * X_PITCH, LANES), jnp.float32),
                        pltpu.VMEM((kp * A_PITCH, LANES), jnp.float32)],
        compiler_params=_compiler_params(("arbitrary", "arbitrary")),
        name="filter_spectrum",
    )(kern, fa, w2)


FILTER_ROW_TILE = 512


def _filter_kernel(bands_ref, w1t_ref, w1c_ref, w1s_ref, b1_ref, w2_ref, b2_ref, w3_ref, b3_ref, w4_ref, freq_ref,
                   absd_ref, o_ref, *, L):
    tr = o_ref.shape[2]
    hp = lax.Precision.HIGHEST
    rho = pl.program_id(0) * tr + lax.broadcasted_iota(jnp.int32, (tr, 1), 0)
    tau = jnp.where(rho < L, rho, 2 * L - rho).astype(jnp.float32)
    t = tau * (1.0 / (L - 1))
    ang = (2.0 * math.pi * tau / L) * bands_ref[...]
    dot = lambda a, b: jnp.dot(a, b, preferred_element_type=jnp.float32, precision=hp)
    fr = freq_ref[...]
    pre = t * w1t_ref[...] + dot(jnp.cos(ang), w1c_ref[...]) - dot(jnp.sin(ang), w1s_ref[...]) + b1_ref[...]
    h = jnp.sin(fr * pre)
    h = jnp.sin(fr * (dot(h, w2_ref[...]) + b2_ref[...]))
    h = jnp.sin(fr * (dot(h, w3_ref[...]) + b3_ref[...]))
    hh = dot(h, w4_ref[...])
    decay = jnp.exp(-t * absd_ref[...])
    fwd = rho < L
    nonzero = rho != L
    w = HYENA_WIDTH
    for o in range(HYENA_ORDER):
        for c in range(HY_SLABS):
            f = hh[:, (2 * o) * w + c * LANES:(2 * o) * w + (c + 1) * LANES]
            b = hh[:, (2 * o + 1) * w + c * LANES:(2 * o + 1) * w + (c + 1) * LANES]
            val = jnp.where(fwd, f, b) * decay[:, c * LANES:(c + 1) * LANES]
            o_ref[o, c] = jnp.where(nonzero, val, 0.0)


def hyena_kernels(L, w1, b1, w2, b2, w3, b3, w4, freq):
    M = 2 * L
    tr = min(FILTER_ROW_TILE, M)
    bands = np.zeros((1, LANES), np.float32)
    bands[0, :FILTER_BANDS] = np.linspace(1e-4, FILTER_BANDS - 1, FILTER_BANDS, dtype=np.float32)
    pad_rows = lambda a: jnp.zeros((LANES, a.shape[1]), jnp.float32).at[:a.shape[0]].set(a)
    min_decay = math.log(DECAY_TARGET) / FAST_DECAY_PCT
    max_decay = math.log(DECAY_TARGET) / SLOW_DECAY_PCT
    absd = np.abs(np.linspace(min_decay, max_decay, HYENA_WIDTH, dtype=np.float32))[None, :]
    args = [jnp.asarray(bands), w1[0:1], pad_rows(w1[1:1 + FILTER_BANDS]), pad_rows(w1[1 + FILTER_BANDS:]),
            b1[None, :], w2, b2[None, :], w3, b3[None, :], w4, freq[None, :], jnp.asarray(absd)]
    full = lambda a: pl.BlockSpec(a.shape, lambda i: (0,) * a.ndim)
    return pl.pallas_call(
        partial(_filter_kernel, L=L),
        grid=(M // tr,),
        in_specs=[full(a) for a in args],
        out_specs=pl.BlockSpec((HYENA_ORDER, HY_SLABS, tr, LANES), lambda i: (0, 0, i, 0)),
        out_shape=jax.ShapeDtypeStruct((HYENA_ORDER, HY_SLABS, M, LANES), jnp.float32),
        compiler_params=_compiler_params(("parallel",)),
        name="hyena_filter",
    )(*args)


def hyena_operator(u, p):
    L = u.shape[2]
    consts = _dft_constants(2 * L)
    kern = hyena_kernels(L, p["filt_w1"], p["filt_b1"], p["filt_w2"], p["filt_b2"], p["filt_w3"], p["filt_b3"],
                         p["filt_w4"], p["filt_freq"])
    kspec = filter_spectrum(kern, consts)
    sw = p["short_w"].reshape(SHORT_CONV * U_SLABS, LANES)
    sb = p["short_b"].reshape(U_SLABS, LANES)
    skip = p["hyena_skip"].reshape(HYENA_ORDER * HY_SLABS, LANES)
    return hyena_mix(u, sw, sb, skip, kspec, consts)


def expert_choice_ffn(x1, h, affinity, wg, wu, wd):
    N, D = h.shape
    cap = EC_CAPACITY * N // N_EXPERTS
    gates, idx = lax.top_k(affinity.T, cap)
    ye = expert_ffn(h[idx], gates[..., None], wg, wu, wd)
    return x1.at[idx.reshape(-1)].add(ye.reshape(-1, D))


def encoder_trunk(x, p):
    B, L, D = x.shape
    qkv, u = in_proj(x, p["norm_mix_g"], p["w_in"])
    attn = dilated_attention(qkv)
    hy = hyena_operator(u, p)
    x1, h, affinity = out_proj(x, attn, hy, p["attn_out_g"], p["hyena_out_g"], p["w_out"], p["norm_ffn_g"],
                               p["w_router"])
    x2 = expert_choice_ffn(x1, h, affinity, p["w_gate"], p["w_up"], p["w_down"])
    return final_norm(x2, p["final_g"]).reshape(B, L, D)


def kernel(x_prompt, x_sample, norm_mix_g, w_in, short_w, short_b, filt_w1, filt_b1, filt_w2, filt_b2, filt_w3, filt_b3, filt_w4, filt_freq, hyena_skip, attn_out_g, hyena_out_g, w_out, norm_ffn_g, w_router, w_gate, w_up, w_down, final_g):
    bf16 = jnp.bfloat16
    p = {
        "norm_mix_g": norm_mix_g, "w_in": w_in[0].astype(bf16),
        "short_w": short_w[0], "short_b": short_b[0],
        "filt_w1": filt_w1[0], "filt_b1": filt_b1[0], "filt_w2": filt_w2[0], "filt_b2": filt_b2[0],
        "filt_w3": filt_w3[0], "filt_b3": filt_b3[0], "filt_w4": filt_w4[0], "filt_freq": filt_freq[0],
        "hyena_skip": hyena_skip[0], "attn_out_g": attn_out_g, "hyena_out_g": hyena_out_g,
        "w_out": w_out[0].astype(bf16), "norm_ffn_g": norm_ffn_g, "w_router": w_router[0],
        "w_gate": w_gate[0].astype(bf16), "w_up": w_up[0].astype(bf16), "w_down": w_down[0].astype(bf16),
        "final_g": final_g.reshape(1, D_MODEL),
    }
    return (encoder_trunk(x_prompt, p), encoder_trunk(x_sample, p))
```

```python
import math
from functools import partial

import jax
import jax.numpy as jnp
import numpy as np
from jax import lax
from jax.experimental import pallas as pl
from jax.experimental.pallas import tpu as pltpu

D_MODEL = 1024
HEAD_DIM = 64
N_ATTN_HEADS = 8
ATTN_WIDTH = N_ATTN_HEADS * HEAD_DIM
HYENA_WIDTH = D_MODEL - ATTN_WIDTH
HYENA_ORDER = 2
SHORT_CONV = 3
FILTER_EMB = 33
FILTER_BANDS = (FILTER_EMB - 1) // 2
FAST_DECAY_PCT = 0.3
SLOW_DECAY_PCT = 1.5
DECAY_TARGET = 1e-2
DILATED_PATTERNS = ((128, 1), (512, 4), (2048, 16))
N_EXPERTS = 16
EC_CAPACITY = 2
D_EXPERT = 2816
NORM_EPS = 1e-6
MASK_VALUE = -1e30

LANES = 128
VMEM_LIMIT_BYTES = 56 * 1024 * 1024

ROW_TILE = 512
FFN_TOKEN_TILE = 1024
FFN_HIDDEN_TILE = 1408
ATTN_Q_TILE = 128
ATTN_TILES_PER_TRIP = 4


def _compiler_params(semantics):
    return pltpu.CompilerParams(dimension_semantics=semantics, vmem_limit_bytes=VMEM_LIMIT_BYTES)


def _rms_scale(x):
    return lax.rsqrt(jnp.mean(x * x, axis=-1, keepdims=True) + NORM_EPS)


def _in_proj_kernel(x_ref, g_ref, w_ref, qkv_ref, u_ref):
    x = x_ref[...]
    h = (x * _rms_scale(x) * g_ref[...]).astype(jnp.bfloat16)
    a = ATTN_WIDTH
    qkv = jnp.dot(h, w_ref[:, :3 * a], preferred_element_type=jnp.float32)
    for s in range(qkv_ref.shape[1]):
        cols = qkv[:, s * LANES:(s + 1) * LANES]
        qkv_ref[0, s] = cols * (HEAD_DIM ** -0.5) if s < a // LANES else cols
    u = jnp.dot(h, w_ref[:, 3 * a:], preferred_element_type=jnp.float32)
    for s in range(u_ref.shape[1]):
        u_ref[0, s] = u[:, s * LANES:(s + 1) * LANES]


def in_proj(x, g, w):
    B, L, D = x.shape
    N = B * L
    tm = min(ROW_TILE, L)
    assert L % tm == 0
    nt = L // tm
    a3, h3 = 3 * ATTN_WIDTH, 3 * HYENA_WIDTH
    slabs = lambda width: pl.BlockSpec((1, width // LANES, tm, LANES), lambda i: (i // nt, 0, i % nt, 0))
    return pl.pallas_call(
        _in_proj_kernel,
        grid=(N // tm,),
        in_specs=[
            pl.BlockSpec((tm, D), lambda i: (i, 0)),
            pl.BlockSpec((1, D), lambda i: (0, 0)),
            pl.BlockSpec((D, a3 + h3), lambda i: (0, 0)),
        ],
        out_specs=[slabs(a3), slabs(h3)],
        out_shape=[jax.ShapeDtypeStruct((B, a3 // LANES, L, LANES), jnp.float32),
                   jax.ShapeDtypeStruct((B, h3 // LANES, L, LANES), jnp.float32)],
        compiler_params=_compiler_params(("parallel",)),
        name="in_proj",
    )(x.reshape(N, D), g, w)


def _alibi_slope(h):
    return 2.0 ** (-8.0 * (h + 1) / N_ATTN_HEADS)


def _attn_tiling(L, window, dilation):
    radius = window // (2 * dilation)
    n = L // dilation
    tq = min(ATTN_Q_TILE, n)
    tk = min(tq + 2 * radius, n)
    return radius, n, tq, tk


def _attn_kernel(q_ref, k_ref, v_ref, o_ref, lse_ref, *bias_refs, L, hp_axis):
    hp = pl.program_id(hp_axis)

    def pick(values):
        out = jnp.float32(values[-1])
        for j in range(len(values) - 2, -1, -1):
            out = jnp.where(hp == j, jnp.float32(values[j]), out)
        return out

    n_pairs = ATTN_WIDTH // LANES
    slope_a = pick([_alibi_slope(2 * j) for j in range(n_pairs)])
    slope_b = pick([_alibi_slope(2 * j + 1) for j in range(n_pairs)])

    for branch, (window, dilation) in enumerate(DILATED_PATTERNS):
        radius, n, tq, tk = _attn_tiling(L, window, dilation)
        tiles = n // tq
        first = branch == 0

        lane = lax.broadcasted_iota(jnp.int32, (tq, LANES), 1)
        low = lane < HEAD_DIM
        row2 = lax.broadcasted_iota(jnp.int32, (2 * tq, tk), 0)
        col2 = lax.broadcasted_iota(jnp.int32, (2 * tq, tk), 1)
        base_rel = col2 - jnp.where(row2 >= tq, row2 - tq, row2)
        second_col = lax.broadcasted_iota(jnp.int32, (2 * tq, 1), 0) >= tq
        slope = jnp.where(second_col, slope_b, slope_a) * float(dilation)

        offsets = {q0 - min(max(q0 - radius, 0), n - tk) for q0 in range(0, n, tq)}
        assert offsets <= {0, radius, 2 * radius}
        bias_ref = bias_refs[branch]
        for case in range(3):
            absrel = jnp.abs(base_rel - case * radius)
            bias_ref[case] = jnp.where(absrel <= radius, -(slope * absrel.astype(jnp.float32)), MASK_VALUE)

        def tile(it, carry, dilation=dilation, radius=radius, n=n, tq=tq, tk=tk, tiles=tiles, first=first,
                 low=low, lane=lane, bias_ref=bias_ref):
            r = it // tiles
            q0 = (it % tiles) * tq
            ks = jnp.clip(q0 - radius, 0, n - tk)
            q_rows = pl.ds(r + dilation * q0, tq, stride=dilation)
            k_rows = pl.ds(r + dilation * ks, tk, stride=dilation)
            q = q_ref[0, 0, q_rows, :].astype(jnp.bfloat16)
            kw = k_ref[0, 0, k_rows, :].astype(jnp.bfloat16)
            vw = v_ref[0, 0, k_rows, :].astype(jnp.bfloat16)
            zero = jnp.zeros_like(q)
            q2 = jnp.concatenate([jnp.where(low, q, zero), jnp.where(low, zero, q)], axis=0)
            s = lax.dot_general(q2, kw, (((1,), (1,)), ((), ())), preferred_element_type=jnp.float32)
            s = s + bias_ref[(q0 - ks) // radius]
            m = jnp.max(s, axis=-1, keepdims=True)
            p = jnp.exp(s - m)
            l = jnp.sum(p, axis=-1, keepdims=True)
            o2 = jnp.dot(p.astype(jnp.bfloat16), vw, preferred_element_type=jnp.float32) / l
            lse2 = m + jnp.log(l)
            o = jnp.where(low, o2[:tq], o2[tq:])
            lse = jnp.where(lane == 0, lse2[:tq], jnp.where(lane == 1, lse2[tq:], 0.0))
            if not first:
                prev = lse_ref[q_rows, :]
                mx = jnp.maximum(prev, lse)
                new = mx + jnp.log(jnp.exp(prev - mx) + jnp.exp(lse - mx))
                w_prev = jnp.exp(prev - new)
                w_cur = jnp.exp(lse - new)
                o = (o_ref[0, 0, q_rows, :] * jnp.where(low, w_prev[:, 0:1], w_prev[:, 1:2])
                     + o * jnp.where(low, w_cur[:, 0:1], w_cur[:, 1:2]))
                lse = new
            return q_rows, o, lse

        def tiles_body(j, carry, tile=tile):
            done = [tile(ATTN_TILES_PER_TRIP * j + i, carry) for i in range(ATTN_TILES_PER_TRIP)]
            for q_rows, o, lse in done:
                o_ref[0, 0, q_rows, :] = o
                lse_ref[q_rows, :] = lse
            return carry

        assert (dilation * tiles) % ATTN_TILES_PER_TRIP == 0
        lax.fori_loop(0, dilation * tiles // ATTN_TILES_PER_TRIP, tiles_body, 0)


def dilated_attention(qkv):
    B, _, L, _ = qkv.shape
    n_pairs = ATTN_WIDTH // LANES
    bias_scratch = []
    for window, dilation in DILATED_PATTERNS:
        radius, n, tq, tk = _attn_tiling(L, window, dilation)
        assert L % dilation == 0 and n % tq == 0
        bias_scratch.append(pltpu.VMEM((3, 2 * tq, tk), jnp.float32))
    spec = lambda g: pl.BlockSpec((1, 1, L, LANES), lambda b, hp: (b, g * n_pairs + hp, 0, 0))
    return pl.pallas_call(
        partial(_attn_kernel, L=L, hp_axis=1),
        grid=(B, n_pairs),
        in_specs=[spec(0), spec(1), spec(2)],
        out_specs=pl.BlockSpec((1, 1, L, LANES), lambda b, hp: (b, hp, 0, 0)),
        out_shape=jax.ShapeDtypeStruct((B, n_pairs, L, LANES), jnp.float32),
        scratch_shapes=[pltpu.VMEM((L, LANES), jnp.float32)] + bias_scratch,
        compiler_params=_compiler_params(("parallel", "arbitrary")),
        name="dilated_attn",
    )(qkv, qkv, qkv)


def _out_proj_kernel(x_ref, attn_ref, hy_ref, ag_ref, hg_ref, w_ref, ng_ref, wr_ref, x1_ref, h_ref, aff_ref):
    a = ATTN_WIDTH

    def group_norm(ref, g_ref):
        v = jnp.concatenate([ref[0, s] for s in range(ref.shape[1])], axis=-1)
        return (v * _rms_scale(v) * g_ref[...]).astype(jnp.bfloat16)

    y = jnp.dot(group_norm(attn_ref, ag_ref), w_ref[:a, :], preferred_element_type=jnp.float32)
    y = y + jnp.dot(group_norm(hy_ref, hg_ref), w_ref[a:, :], preferred_element_type=jnp.float32)
    x1 = x_ref[...] + y
    x1_ref[...] = x1
    h = x1 * _rms_scale(x1) * ng_ref[...]
    h_hi = h.astype(jnp.bfloat16)
    h_ref[...] = h_hi
    h_lo = (h - h_hi.astype(jnp.float32)).astype(jnp.bfloat16)
    wr = wr_ref[...]
    wr_hi = wr.astype(jnp.bfloat16)
    wr_lo = (wr - wr_hi.astype(jnp.float32)).astype(jnp.bfloat16)
    dot = lambda p, q: jnp.dot(p, q, preferred_element_type=jnp.float32)
    logits = dot(h_hi, wr_hi) + (dot(h_lo, wr_hi) + dot(h_hi, wr_lo))
    e = jnp.exp(logits - jnp.max(logits, axis=-1, keepdims=True))
    aff_ref[...] = e / jnp.sum(e, axis=-1, keepdims=True)


def out_proj(x, attn, hy, attn_g, hy_g, w, ffn_g, w_router):
    B, L, D = x.shape
    N = B * L
    tm = min(ROW_TILE, L)
    assert L % tm == 0
    nt = L // tm
    E = w_router.shape[-1]
    slabs = lambda width: pl.BlockSpec((1, width // LANES, tm, LANES), lambda i: (i // nt, 0, i % nt, 0))
    row = lambda width: pl.BlockSpec((tm, width), lambda i: (i, 0))
    full = lambda r, c: pl.BlockSpec((r, c), lambda i: (0, 0))
    return pl.pallas_call(
        _out_proj_kernel,
        grid=(N // tm,),
        in_specs=[row(D), slabs(ATTN_WIDTH), slabs(HYENA_WIDTH), full(1, ATTN_WIDTH), full(1, HYENA_WIDTH),
                  full(D, D), full(1, D), full(D, E)],
        out_specs=[row(D), row(D), row(E)],
        out_shape=[jax.ShapeDtypeStruct((N, D), jnp.float32), jax.ShapeDtypeStruct((N, D), jnp.bfloat16),
                   jax.ShapeDtypeStruct((N, E), jnp.float32)],
        compiler_params=_compiler_params(("parallel",)),
        name="out_proj",
    )(x.reshape(N, D), attn, hy, attn_g, hy_g, w, ffn_g, w_router)


def _ffn_kernel(x_ref, gate_ref, wg_ref, wu_ref, wd_ref, o_ref, acc_ref):
    f = pl.program_id(2)
    x = x_ref[0]
    a = jnp.dot(x, wg_ref[0], preferred_element_type=jnp.float32)
    b = jnp.dot(x, wu_ref[0], preferred_element_type=jnp.float32)
    h = (a * jax.nn.sigmoid(a) * b).astype(jnp.bfloat16)
    y = jnp.dot(h, wd_ref[0], preferred_element_type=jnp.float32)

    @pl.when(f == 0)
    def _():
        acc_ref[...] = y

    @pl.when(f > 0)
    def _():
        acc_ref[...] += y

    @pl.when(f == pl.num_programs(2) - 1)
    def _():
        o_ref[0] = (acc_ref[...] * gate_ref[0]).astype(o_ref.dtype)


def expert_ffn(xe, gates, wg, wu, wd):
    E, C, D = xe.shape
    F = wg.shape[-1]
    tm = min(FFN_TOKEN_TILE, C)
    tf = min(FFN_HIDDEN_TILE, F)
    assert C % tm == 0 and F % tf == 0
    return pl.pallas_call(
        _ffn_kernel,
        grid=(E, C // tm, F // tf),
        in_specs=[
            pl.BlockSpec((1, tm, D), lambda e, i, f: (e, i, 0)),
            pl.BlockSpec((1, tm, 1), lambda e, i, f: (e, i, 0)),
            pl.BlockSpec((1, D, tf), lambda e, i, f: (e, 0, f)),
            pl.BlockSpec((1, D, tf), lambda e, i, f: (e, 0, f)),
            pl.BlockSpec((1, tf, D), lambda e, i, f: (e, f, 0)),
        ],
        out_specs=pl.BlockSpec((1, tm, D), lambda e, i, f: (e, i, 0)),
        out_shape=jax.ShapeDtypeStruct((E, C, D), jnp.bfloat16),
        scratch_shapes=[pltpu.VMEM((tm, D), jnp.float32)],
        compiler_params=_compiler_params(("parallel", "parallel", "arbitrary")),
        name="moe_ffn",
    )(xe, gates, wg, wu, wd)


ROUTE_CHUNK = LANES
ROUTE_WALKS = 4
COMBINE_TOKENS = 512
COMBINE_WINDOW = 128


def _route_select_kernel(aff_ref, tri_ref, ones_ref, low_ref, slot_ref, rank_ref, *, cap):
    a = aff_ref[0]

    def as_float(bits):
        return pltpu.bitcast(jnp.full((8, LANES), bits, jnp.int32), jnp.float32)[0:1, :]

    def bisect(_, lohi):
        lo, hi = lohi
        mid = lo + (hi - lo + 1) // 2
        ok = jnp.sum((a >= as_float(mid)).astype(jnp.int32)) >= cap
        return jnp.where(ok, mid, lo), jnp.where(ok, hi, mid - 1)

    thr_bits, _ = lax.fori_loop(0, 31, bisect, (jnp.int32(0), jnp.int32(0x7F800000)))
    thr = as_float(thr_bits)

    def exclusive_prefix(m):
        x = jnp.where(m, 1.0, 0.0).astype(jnp.bfloat16)
        incl = jnp.dot(x, tri_ref[...], preferred_element_type=jnp.float32)
        tot = jnp.dot(x, ones_ref[...], preferred_element_type=jnp.float32).astype(jnp.bfloat16)
        offs = jnp.dot(low_ref[...], tot, preferred_element_type=jnp.float32)
        return offs + incl - x.astype(jnp.float32)

    gt = a > thr
    eq = a == thr
    need = (cap - jnp.sum(gt.astype(jnp.int32))).astype(jnp.float32)
    sel = gt | (eq & (exclusive_prefix(eq) < need))
    rank = exclusive_prefix(sel).astype(jnp.int32)
    rank_ref[0] = rank
    slot_ref[0] = jnp.where(sel, rank, -1)


def route_select(aff3, cap):
    E, nc, w = aff3.shape
    tri = jnp.asarray(np.triu(np.ones((w, w), np.float32)), jnp.bfloat16)
    ones = jnp.ones((w, w), jnp.bfloat16)
    low = jnp.asarray(np.tril(np.ones((nc, nc), np.float32), -1), jnp.bfloat16)
    blk = pl.BlockSpec((1, nc, w), lambda e: (e, 0, 0))
    full = lambda a: pl.BlockSpec(a.shape, lambda e: (0,) * a.ndim)
    out = jax.ShapeDtypeStruct((E, nc, w), jnp.int32)
    return pl.pallas_call(
        partial(_route_select_kernel, cap=cap),
        grid=(E,),
        in_specs=[blk, full(tri), full(ones), full(low)],
        out_specs=[blk, blk],
        out_shape=[out, out],
        compiler_params=_compiler_params(("parallel",)),
        name="route_select",
    )(aff3, tri, ones, low)


def _route_compact_kernel(offs_ref, slot_ref, aff_ref, idx_ref, gate_ref, acc_ref, *, cap):
    e = pl.program_id(0)
    w = ROUTE_CHUNK
    nc = slot_ref.shape[1]
    span = nc // ROUTE_WALKS
    last_block = (cap - 1) // w
    sub = lax.broadcasted_iota(jnp.int32, (2 * w, w), 0)
    lane = lax.broadcasted_iota(jnp.int32, (1, w), 1)
    zero_row = jnp.zeros((1, w), jnp.float32)

    def block_of(c):
        return jnp.where(c < nc, jnp.minimum(offs_ref[e, jnp.minimum(c, nc - 1)] // w, last_block), last_block)

    def store(walk, block, rows):
        acc_ref[walk, 0, pl.ds(block, 1), :] = rows[0:1] * 256.0 + rows[1:2]
        acc_ref[walk, 1, pl.ds(block, 1), :] = rows[2:3] + rows[3:4] + rows[4:5]

    def chunk(walk, c, carry):
        b0 = block_of(c)
        rel = slot_ref[0, pl.ds(c, 1), :] - b0 * w
        onehot = jnp.where(sub == rel, 1.0, 0.0).astype(jnp.bfloat16)
        tok = c * w + lane
        g = aff_ref[0, pl.ds(c, 1), :]
        g_hi = g.astype(jnp.bfloat16).astype(jnp.float32)
        g_mid = (g - g_hi).astype(jnp.bfloat16).astype(jnp.float32)
        g_lo = g - g_hi - g_mid
        vals = jnp.concatenate([(tok >> 8).astype(jnp.float32), (tok & 255).astype(jnp.float32), g_hi, g_mid,
                                g_lo, zero_row, zero_row, zero_row], axis=0).astype(jnp.bfloat16)
        res = lax.dot_general(vals, onehot, (((1,), (1,)), ((), ())), preferred_element_type=jnp.float32)
        total = carry + res[:, :w]
        store(walk, b0, total)
        return jnp.where(block_of(c + 1) == b0, total, res[:, w:])

    acc_ref[...] = jnp.zeros_like(acc_ref)

    def step(j, carries):
        return tuple(chunk(q, q * span + j, carries[q]) for q in range(ROUTE_WALKS))

    zeros = jnp.zeros((8, w), jnp.float32)
    carries = lax.fori_loop(0, span, step, (zeros,) * ROUTE_WALKS, unroll=2)
    for q in range(ROUTE_WALKS):
        store(q, block_of((q + 1) * span), carries[q])
    idx_ref[0] = sum(acc_ref[q, 0] for q in range(ROUTE_WALKS)).astype(jnp.int32)
    gate_ref[0] = sum(acc_ref[q, 1] for q in range(ROUTE_WALKS))


def route_compact(slot3, aff3, offs, cap):
    E, nc, w = slot3.shape
    assert cap % w == 0 and nc % ROUTE_WALKS == 0
    blk = pl.BlockSpec((1, nc, w), lambda e, offs: (e, 0, 0))
    out_blk = pl.BlockSpec((1, cap // w, w), lambda e, offs: (e, 0, 0))
    idx, gates = pl.pallas_call(
        partial(_route_compact_kernel, cap=cap),
        grid_spec=pltpu.PrefetchScalarGridSpec(
            num_scalar_prefetch=1, grid=(E,), in_specs=[blk, blk], out_specs=[out_blk, out_blk],
            scratch_shapes=[pltpu.VMEM((ROUTE_WALKS, 2, cap // w, w), jnp.float32)]),
        out_shape=[jax.ShapeDtypeStruct((E, cap // w, w), jnp.int32),
                   jax.ShapeDtypeStruct((E, cap // w, w), jnp.float32)],
        compiler_params=_compiler_params(("arbitrary",)),
        name="route_compact",
    )(offs, slot3, aff3)
    return idx.reshape(E, cap), gates.reshape(E, cap)


def _combine_kernel(starts_ref, x_ref, slot_ref, g_ref, ye_hbm, o_ref, stage_ref, y_ref, sems, *, cap):
    j = pl.program_id(0)
    n_tiles = pl.num_programs(0)
    n_experts = slot_ref.shape[1]
    t = x_ref.shape[0]
    win = COMBINE_WINDOW
    buf = j % 2

    def window_copy(b, e, start):
        return pltpu.make_async_copy(ye_hbm.at[e, pl.ds(start, win), :], stage_ref.at[b, pl.ds(e * win, win), :],
                                     sems.at[b, e])

    def aligned(start):
        return pl.multiple_of(jnp.minimum((start // 16) * 16, cap - win), 16)

    def first_windows(tile):
        return [aligned(starts_ref[e, tile]) for e in range(n_experts)]

    @pl.when(j == 0)
    def _():
        for e, start in enumerate(first_windows(0)):
            window_copy(0, e, start).start()

    @pl.when(j + 1 < n_tiles)
    def _():
        for e, start in enumerate(first_windows(j + 1)):
            window_copy(1 - buf, e, start).start()

    bases = first_windows(j)
    lane = lax.broadcasted_iota(jnp.int32, (t, win), 1)
    slot = slot_ref[...]
    onehot = jnp.concatenate(
        [jnp.where(slot[:, e:e + 1] - bases[e] == lane, 1.0, 0.0).astype(jnp.bfloat16) for e in range(n_experts)],
        axis=1)
    for e in range(n_experts):
        window_copy(buf, e, bases[e]).wait()
    y_ref[...] = jnp.dot(onehot, stage_ref[buf], preferred_element_type=jnp.float32)

    for e in range(n_experts):
        end = starts_ref[e, j + 1]
        n_more = jnp.maximum(end - bases[e] + win - 1, 0) // win - 1

        def more(k, carry, e=e):
            first = bases[e] + (k + 1) * win
            start = aligned(first)
            cp = window_copy(buf, e, start)
            cp.start()
            cp.wait()
            col = slot_ref[:, e:e + 1]
            hit = (col - start == lane) & (col >= first)
            rows = stage_ref[buf, pl.ds(e * win, win), :]
            y_ref[...] += jnp.dot(jnp.where(hit, 1.0, 0.0).astype(jnp.bfloat16), rows,
                                  preferred_element_type=jnp.float32)
            return carry

        lax.fori_loop(0, n_more, more, 0)

    x = x_ref[...] + y_ref[...]
    o_ref[...] = x * _rms_scale(x) * g_ref[...]


def combine(x1, slot_t, starts, ye, g):
    N, D = x1.shape
    E, cap, _ = ye.shape
    t = min(COMBINE_TOKENS, N)
    assert N % t == 0 and cap % COMBINE_WINDOW == 0
    return pl.pallas_call(
        partial(_combine_kernel, cap=cap),
        grid_spec=pltpu.PrefetchScalarGridSpec(
            num_scalar_prefetch=1, grid=(N // t,),
            in_specs=[pl.BlockSpec((t, D), lambda j, s: (j, 0)), pl.BlockSpec((t, E), lambda j, s: (j, 0)),
                      pl.BlockSpec((1, D), lambda j, s: (0, 0)), pl.BlockSpec(memory_space=pl.ANY)],
            out_specs=pl.BlockSpec((t, D), lambda j, s: (j, 0)),
            scratch_shapes=[pltpu.VMEM((2, E * COMBINE_WINDOW, D), jnp.bfloat16), pltpu.VMEM((t, D), jnp.float32),
                            pltpu.SemaphoreType.DMA((2, E))]),
        out_shape=jax.ShapeDtypeStruct((N, D), jnp.float32),
        compiler_params=_compiler_params(("arbitrary",)),
        name="moe_combine",
    )(starts, x1, slot_t, g, ye)


DFT_P = 128
SLAB_PAD = 8
X_PITCH = DFT_P + SLAB_PAD
A_PITCH = 2 * DFT_P + SLAB_PAD
HY_SLABS = HYENA_WIDTH // LANES
U_SLABS = 3 * HY_SLABS
SLOW_STAGE_UNROLL = 8
SPECTRAL_UNROLL = 4


def _dft_sizes(M):
    q = M // DFT_P
    ka = q // 2 + 1
    kp = -(-ka // 8) * 8
    return q, ka, kp


def _dft_constants(M):
    p = DFT_P
    q, ka, kp = _dft_sizes(M)
    k = np.arange(ka)

    def slow_fwd(n_ts):
        ang = 2.0 * np.pi * ((k[:, None] * np.arange(n_ts)[None, :]) % q) / q
        m = np.zeros((2 * kp, n_ts))
        m[:ka] = np.cos(ang)
        m[kp:kp + ka] = -np.sin(ang)
        return m

    ts = np.arange(q // 2)
    ang = 2.0 * np.pi * ((ts[:, None] * k[None, :]) % q) / q
    c = np.where((k == 0) | (k == q // 2), 1.0, 2.0)[None, :]
    slow_inv = np.zeros((q // 2, 2 * kp))
    slow_inv[:, :ka] = c * np.cos(ang) / M
    slow_inv[:, kp:kp + ka] = -c * np.sin(ang) / M

    kb = np.arange(p)
    tf = np.arange(p)
    idx = (tf[None, None, :] * (k[:, None, None] + q * kb[None, :, None])) % M
    wr = np.cos(2.0 * np.pi * idx / M)
    wi = -np.sin(2.0 * np.pi * idx / M)
    w2 = np.concatenate([np.concatenate([wr, -wi], axis=2), np.concatenate([wi, wr], axis=2)], axis=1)
    bf = lambda a: jnp.asarray(a, jnp.float32).astype(jnp.bfloat16)
    return {"fa_half": bf(slow_fwd(q // 2)), "fa_full": bf(slow_fwd(q)), "fi": bf(slow_inv),
            "w2": bf(w2), "v2": bf(np.transpose(w2, (0, 2, 1)))}


def _slow_stage_forward(xs_ref, ag_ref, fa_ref, n_ts, kp):
    fa = fa_ref[...]

    def body(tf, carry):
        xt = xs_ref[pl.ds(tf, n_ts, stride=X_PITCH), :].astype(jnp.bfloat16)
        a = jnp.dot(fa, xt, preferred_element_type=jnp.float32)
        ag_ref[pl.ds(tf, kp, stride=A_PITCH), :] = a[:kp]
        ag_ref[pl.ds(DFT_P + tf, kp, stride=A_PITCH), :] = a[kp:]
        return carry

    lax.fori_loop(0, DFT_P, body, 0, unroll=SLOW_STAGE_UNROLL)


def _slow_stage_inverse(ag_ref, ys_ref, fi_ref, n_ts, kp):
    fi = fi_ref[...]

    def body(tf, carry):
        gr = ag_ref[pl.ds(tf, kp, stride=A_PITCH), :]
        gi = ag_ref[pl.ds(DFT_P + tf, kp, stride=A_PITCH), :]
        g = jnp.concatenate([gr, gi], axis=0).astype(jnp.bfloat16)
        ys_ref[pl.ds(tf, n_ts, stride=X_PITCH), :] = jnp.dot(fi, g, preferred_element_type=jnp.float32)
        return carry

    lax.fori_loop(0, DFT_P, body, 0, unroll=SLOW_STAGE_UNROLL)


def _fast_stage(ag_ref, w2_ref, k):
    r0 = pl.multiple_of(k * A_PITCH, 8)
    a2 = ag_ref[pl.ds(r0, 2 * DFT_P), :].astype(jnp.bfloat16)
    return r0, jnp.dot(w2_ref[k], a2, preferred_element_type=jnp.float32)


def _spectral_product(ag_ref, w2_ref, v2_ref, kspec_ref, ka):
    p = DFT_P

    def block(k):
        r0, x2 = _fast_stage(ag_ref, w2_ref, k)
        kk = kspec_ref[0, 0, pl.ds(pl.multiple_of(k * 2 * p, 2 * p), 2 * p), :]
        xr, xi, kr, ki = x2[:p], x2[p:], kk[:p], kk[p:]
        y2 = jnp.concatenate([xr * kr - xi * ki, xr * ki + xi * kr], axis=0).astype(jnp.bfloat16)
        return r0, jnp.dot(v2_ref[k], y2, preferred_element_type=jnp.float32)

    def body(j, carry):
        done = [block(SPECTRAL_UNROLL * j + i) for i in range(SPECTRAL_UNROLL)]
        for r0, g2 in done:
            ag_ref[pl.ds(r0, 2 * p), :] = g2
        return carry

    lax.fori_loop(0, ka // SPECTRAL_UNROLL, body, 0)
    for k in range(ka - ka % SPECTRAL_UNROLL, ka):
        r0, g2 = block(k)
        ag_ref[pl.ds(r0, 2 * p), :] = g2


def _hyena_kernel(uv_ref, u1_ref, u2_ref, sw_ref, sb_ref, skip_ref, k0_ref, k1_ref, fa_ref, fi_ref, w2_ref,
                  v2_ref, o_ref, xs_ref, ys_ref, ag_ref, *, L):
    p = DFT_P
    c = pl.program_id(0)
    n_ts = L // p
    _, ka, kp = _dft_sizes(2 * L)
    row = lax.broadcasted_iota(jnp.int32, (p, LANES), 0)

    def short_conv_block(u_ref, group, ts):
        r0 = pl.multiple_of(ts * p, p)
        mid = u_ref[0, 0, pl.ds(r0, p), :]
        prev8 = u_ref[0, 0, pl.ds(pl.multiple_of(jnp.maximum(r0 - 8, 0), 8), 8), :]
        next8 = u_ref[0, 0, pl.ds(pl.multiple_of(jnp.minimum(r0 + p, L - 8), 8), 8), :]
        has_prev = (r0 > 0).astype(jnp.float32)
        has_next = (r0 + p < L).astype(jnp.float32)
        up = jnp.where(row == 0, prev8[7:8, :] * has_prev, pltpu.roll(mid, 1, 0))
        dn = jnp.where(row == p - 1, next8[0:1, :] * has_next, pltpu.roll(mid, p - 1, 0))
        ch = group * HY_SLABS + c
        w = lambda tap: sw_ref[pl.ds(tap * U_SLABS + ch, 1), :]
        return up * w(0) + mid * w(1) + dn * w(2) + sb_ref[pl.ds(ch, 1), :]

    def long_conv(kspec_ref):
        _slow_stage_forward(xs_ref, ag_ref, fa_ref, n_ts, kp)
        _spectral_product(ag_ref, w2_ref, v2_ref, kspec_ref, ka)
        _slow_stage_inverse(ag_ref, ys_ref, fi_ref, n_ts, kp)

    def blocks(fn):
        def body(ts, carry):
            fn(ts, pl.ds(pl.multiple_of(ts * X_PITCH, 8), p))
            return carry
        lax.fori_loop(0, n_ts, body, 0)

    def stage_v(ts, rows):
        xs_ref[rows, :] = short_conv_block(uv_ref, 0, ts)

    def stage_z(ts, rows):
        v = xs_ref[rows, :]
        y1 = ys_ref[rows, :] + skip_ref[pl.ds(c, 1), :] * v
        xs_ref[rows, :] = short_conv_block(u1_ref, 1, ts) * y1

    def stage_out(ts, rows):
        z = xs_ref[rows, :]
        y2 = ys_ref[rows, :] + skip_ref[pl.ds(HY_SLABS + c, 1), :] * z
        o_ref[0, 0, pl.ds(pl.multiple_of(ts * p, p), p), :] = short_conv_block(u2_ref, 2, ts) * y2

    blocks(stage_v)
    long_conv(k0_ref)
    blocks(stage_z)
    long_conv(k1_ref)
    blocks(stage_out)


def hyena_mix(u, sw, sb, skip, kspec, consts):
    B, n_slabs, L, _ = u.shape
    q, ka, kp = _dft_sizes(2 * L)
    n_ts = L // DFT_P
    u_spec = lambda g: pl.BlockSpec((1, 1, L, LANES), lambda c, b: (b, g * HY_SLABS + c, 0, 0))
    k_spec = lambda o: pl.BlockSpec((1, 1, ka * 2 * DFT_P, LANES), lambda c, b: (o, c, 0, 0),
                                    pipeline_mode=pl.Buffered(1))
    const = lambda a: pl.BlockSpec(a.shape, lambda c, b: (0,) * a.ndim, pipeline_mode=pl.Buffered(1))
    small = lambda a: pl.BlockSpec(a.shape, lambda c, b: (0,) * a.ndim)
    fa, fi, w2, v2 = consts["fa_half"], consts["fi"], consts["w2"], consts["v2"]
    return pl.pallas_call(
        partial(_hyena_kernel, L=L),
        grid=(HY_SLABS, B),
        in_specs=[u_spec(0), u_spec(1), u_spec(2), small(sw), small(sb), small(skip), k_spec(0), k_spec(1),
                  small(fa), small(fi), const(w2), const(v2)],
        out_specs=pl.BlockSpec((1, 1, L, LANES), lambda c, b: (b, c, 0, 0)),
        out_shape=jax.ShapeDtypeStruct((B, HY_SLABS, L, LANES), jnp.float32),
        scratch_shapes=[pltpu.VMEM((n_ts * X_PITCH, LANES), jnp.float32),
                        pltpu.VMEM((n_ts * X_PITCH, LANES), jnp.float32),
                        pltpu.VMEM((kp * A_PITCH, LANES), jnp.float32)],
        compiler_params=_compiler_params(("arbitrary", "arbitrary")),
        name="hyena_mix",
    )(u, u, u, sw, sb, skip, kspec, kspec, fa, fi, w2, v2)


def _filter_spectrum_kernel(kern_ref, fa_ref, w2_ref, o_ref, xs_ref, ag_ref, *, M):
    p = DFT_P
    q, ka, kp = _dft_sizes(M)

    def load(ts, carry):
        xs_ref[pl.ds(pl.multiple_of(ts * X_PITCH, 8), p), :] = kern_ref[0, 0, pl.ds(pl.multiple_of(ts * p, p), p), :]
        return carry

    lax.fori_loop(0, q, load, 0)
    _slow_stage_forward(xs_ref, ag_ref, fa_ref, q, kp)

    def body(k, carry):
        _, x2 = _fast_stage(ag_ref, w2_ref, k)
        o_ref[0, 0, pl.ds(pl.multiple_of(k * 2 * p, 2 * p), 2 * p), :] = x2
        return carry

    lax.fori_loop(0, ka, body, 0)


def filter_spectrum(kern, consts):
    n_o, n_c, M, _ = kern.shape
    q, ka, kp = _dft_sizes(M)
    fa, w2 = consts["fa_full"], consts["w2"]
    full = lambda a: pl.BlockSpec(a.shape, lambda o, c: (0,) * a.ndim)
    return pl.pallas_call(
        partial(_filter_spectrum_kernel, M=M),
        grid=(n_o, n_c),
        in_specs=[pl.BlockSpec((1, 1, M, LANES), lambda o, c: (o, c, 0, 0)), full(fa), full(w2)],
        out_specs=pl.BlockSpec((1, 1, ka * 2 * DFT_P, LANES), lambda o, c: (o, c, 0, 0)),
        out_shape=jax.ShapeDtypeStruct((n_o, n_c, ka * 2 * DFT_P, LANES), jnp.float32),
        scratch_shapes=[pltpu.VMEM((q * X_PITCH, LANES), jnp.float32),
                        pltpu.VMEM((kp * A_PITCH, LANES), jnp.float32)],
        compiler_params=_compiler_params(("arbitrary", "arbitrary")),
        name="filter_spectrum",
    )(kern, fa, w2)


FILTER_ROW_TILE = 512


def _filter_kernel(bands_ref, w1t_ref, w1c_ref, w1s_ref, b1_ref, w2_ref, b2_ref, w3_ref, b3_ref, w4_ref, freq_ref,
                   absd_ref, o_ref, *, L):
    tr = o_ref.shape[2]
    hp = lax.Precision.HIGHEST
    rho = pl.program_id(0) * tr + lax.broadcasted_iota(jnp.int32, (tr, 1), 0)
    tau = jnp.where(rho < L, rho, 2 * L - rho).astype(jnp.float32)
    t = tau * (1.0 / (L - 1))
    ang = (2.0 * math.pi * tau / L) * bands_ref[...]
    dot = lambda a, b: jnp.dot(a, b, preferred_element_type=jnp.float32, precision=hp)
    fr = freq_ref[...]
    pre = t * w1t_ref[...] + dot(jnp.cos(ang), w1c_ref[...]) - dot(jnp.sin(ang), w1s_ref[...]) + b1_ref[...]
    h = jnp.sin(fr * pre)
    h = jnp.sin(fr * (dot(h, w2_ref[...]) + b2_ref[...]))
    h = jnp.sin(fr * (dot(h, w3_ref[...]) + b3_ref[...]))
    hh = dot(h, w4_ref[...])
    decay = jnp.exp(-t * absd_ref[...])
    fwd = rho < L
    nonzero = rho != L
    w = HYENA_WIDTH
    for o in range(HYENA_ORDER):
        for c in range(HY_SLABS):
            f = hh[:, (2 * o) * w + c * LANES:(2 * o) * w + (c + 1) * LANES]
            b = hh[:, (2 * o + 1) * w + c * LANES:(2 * o + 1) * w + (c + 1) * LANES]
            val = jnp.where(fwd, f, b) * decay[:, c * LANES:(c + 1) * LANES]
            o_ref[o, c] = jnp.where(nonzero, val, 0.0)


def hyena_kernels(L, w1, b1, w2, b2, w3, b3, w4, freq):
    M = 2 * L
    tr = min(FILTER_ROW_TILE, M)
    bands = np.zeros((1, LANES), np.float32)
    bands[0, :FILTER_BANDS] = np.linspace(1e-4, FILTER_BANDS - 1, FILTER_BANDS, dtype=np.float32)
    pad_rows = lambda a: jnp.zeros((LANES, a.shape[1]), jnp.float32).at[:a.shape[0]].set(a)
    min_decay = math.log(DECAY_TARGET) / FAST_DECAY_PCT
    max_decay = math.log(DECAY_TARGET) / SLOW_DECAY_PCT
    absd = np.abs(np.linspace(min_decay, max_decay, HYENA_WIDTH, dtype=np.float32))[None, :]
    args = [jnp.asarray(bands), w1[0:1], pad_rows(w1[1:1 + FILTER_BANDS]), pad_rows(w1[1 + FILTER_BANDS:]),
            b1[None, :], w2, b2[None, :], w3, b3[None, :], w4, freq[None, :], jnp.asarray(absd)]
    full = lambda a: pl.BlockSpec(a.shape, lambda i: (0,) * a.ndim)
    return pl.pallas_call(
        partial(_filter_kernel, L=L),
        grid=(M // tr,),
        in_specs=[full(a) for a in args],
        out_specs=pl.BlockSpec((HYENA_ORDER, HY_SLABS, tr, LANES), lambda i: (0, 0, i, 0)),
        out_shape=jax.ShapeDtypeStruct((HYENA_ORDER, HY_SLABS, M, LANES), jnp.float32),
        compiler_params=_compiler_params(("parallel",)),
        name="hyena_filter",
    )(*args)


def hyena_operator(u, p):
    L = u.shape[2]
    consts = _dft_constants(2 * L)
    kern = hyena_kernels(L, p["filt_w1"], p["filt_b1"], p["filt_w2"], p["filt_b2"], p["filt_w3"], p["filt_b3"],
                         p["filt_w4"], p["filt_freq"])
    kspec = filter_spectrum(kern, consts)
    sw = p["short_w"].reshape(SHORT_CONV * U_SLABS, LANES)
    sb = p["short_b"].reshape(U_SLABS, LANES)
    skip = p["hyena_skip"].reshape(HYENA_ORDER * HY_SLABS, LANES)
    return hyena_mix(u, sw, sb, skip, kspec, consts)


def expert_choice_ffn(x1, h, affinity, p):
    N, D = h.shape
    E = affinity.shape[1]
    cap = EC_CAPACITY * N // N_EXPERTS
    aff3 = affinity.T.reshape(E, N // ROUTE_CHUNK, ROUTE_CHUNK)
    slot3, rank3 = route_select(aff3, cap)
    idx, gates = route_compact(slot3, aff3, rank3[:, :, 0], cap)
    ye = expert_ffn(h[idx], gates[..., None], p["w_gate"], p["w_up"], p["w_down"])
    t = min(COMBINE_TOKENS, N)
    starts = jnp.concatenate([rank3.reshape(E, N)[:, ::t], jnp.full((E, 1), cap, jnp.int32)], axis=1)
    return combine(x1, slot3.reshape(E, N).T, starts, ye, p["final_g"])


def encoder_trunk(x, p):
    B, L, D = x.shape
    qkv, u = in_proj(x, p["norm_mix_g"], p["w_in"])
    attn = dilated_attention(qkv)
    hy = hyena_operator(u, p)
    x1, h, affinity = out_proj(x, attn, hy, p["attn_out_g"], p["hyena_out_g"], p["w_out"], p["norm_ffn_g"],
                               p["w_router"])
    return expert_choice_ffn(x1, h, affinity, p).reshape(B, L, D)


def kernel(x_prompt, x_sample, norm_mix_g, w_in, short_w, short_b, filt_w1, filt_b1, filt_w2, filt_b2, filt_w3, filt_b3, filt_w4, filt_freq, hyena_skip, attn_out_g, hyena_out_g, w_out, norm_ffn_g, w_router, w_gate, w_up, w_down, final_g):
    bf16 = jnp.bfloat16
    p = {
        "norm_mix_g": norm_mix_g, "w_in": w_in[0].astype(bf16),
        "short_w": short_w[0], "short_b": short_b[0],
        "filt_w1": filt_w1[0], "filt_b1": filt_b1[0], "filt_w2": filt_w2[0], "filt_b2": filt_b2[0],
        "filt_w3": filt_w3[0], "filt_b3": filt_b3[0], "filt_w4": filt_w4[0], "filt_freq": filt_freq[0],
        "hyena_skip": hyena_skip[0], "attn_out_g": attn_out_g, "hyena_out_g": hyena_out_g,
        "w_out": w_out[0].astype(bf16), "norm_ffn_g": norm_ffn_g, "w_router": w_router[0],
        "w_gate": w_gate[0].astype(bf16), "w_up": w_up[0].astype(bf16), "w_down": w_down[0].astype(bf16),
        "final_g": final_g.reshape(1, D_MODEL),
    }
    return (encoder_trunk(x_prompt, p), encoder_trunk(x_sample, p))
```

```python
import math
from functools import partial

import jax
import jax.numpy as jnp
import numpy as np
from jax import lax
from jax.experimental import pallas as pl
from jax.experimental.pallas import tpu as pltpu

D_MODEL = 1024
HEAD_DIM = 64
N_ATTN_HEADS = 8
ATTN_WIDTH = N_ATTN_HEADS * HEAD_DIM
HYENA_WIDTH = D_MODEL - ATTN_WIDTH
HYENA_ORDER = 2
SHORT_CONV = 3
FILTER_EMB = 33
FILTER_BANDS = (FILTER_EMB - 1) // 2
FAST_DECAY_PCT = 0.3
SLOW_DECAY_PCT = 1.5
DECAY_TARGET = 1e-2
DILATED_PATTERNS = ((128, 1), (512, 4), (2048, 16))
N_EXPERTS = 16
EC_CAPACITY = 2
D_EXPERT = 2816
NORM_EPS = 1e-6
MASK_VALUE = -1e30

LANES = 128
VMEM_LIMIT_BYTES = 56 * 1024 * 1024

ROW_TILE = 512
FFN_TOKEN_TILE = 1024
FFN_HIDDEN_TILE = 1408
ATTN_Q_TILE = 128
ATTN_TILES_PER_TRIP = 8


def _compiler_params(semantics):
    return pltpu.CompilerParams(dimension_semantics=semantics, vmem_limit_bytes=VMEM_LIMIT_BYTES)


def _rms_scale(x):
    return lax.rsqrt(jnp.mean(x * x, axis=-1, keepdims=True) + NORM_EPS)


def _in_proj_kernel(x_ref, g_ref, w_ref, qkv_ref, u_ref):
    x = x_ref[...]
    h = (x * _rms_scale(x) * g_ref[...]).astype(jnp.bfloat16)
    a = ATTN_WIDTH
    qkv = jnp.dot(h, w_ref[:, :3 * a], preferred_element_type=jnp.float32)
    for s in range(qkv_ref.shape[1]):
        cols = qkv[:, s * LANES:(s + 1) * LANES]
        qkv_ref[0, s] = cols * (HEAD_DIM ** -0.5) if s < a // LANES else cols
    u = jnp.dot(h, w_ref[:, 3 * a:], preferred_element_type=jnp.float32)
    for s in range(u_ref.shape[1]):
        u_ref[0, s] = u[:, s * LANES:(s + 1) * LANES]


def in_proj(x, g, w):
    B, L, D = x.shape
    N = B * L
    tm = min(ROW_TILE, L)
    assert L % tm == 0
    nt = L // tm
    a3, h3 = 3 * ATTN_WIDTH, 3 * HYENA_WIDTH
    slabs = lambda width: pl.BlockSpec((1, width // LANES, tm, LANES), lambda i: (i // nt, 0, i % nt, 0))
    return pl.pallas_call(
        _in_proj_kernel,
        grid=(N // tm,),
        in_specs=[
            pl.BlockSpec((tm, D), lambda i: (i, 0)),
            pl.BlockSpec((1, D), lambda i: (0, 0)),
            pl.BlockSpec((D, a3 + h3), lambda i: (0, 0)),
        ],
        out_specs=[slabs(a3), slabs(h3)],
        out_shape=[jax.ShapeDtypeStruct((B, a3 // LANES, L, LANES), jnp.float32),
                   jax.ShapeDtypeStruct((B, h3 // LANES, L, LANES), jnp.float32)],
        compiler_params=_compiler_params(("parallel",)),
        name="in_proj",
    )(x.reshape(N, D), g, w)


def _alibi_slope(h):
    return 2.0 ** (-8.0 * (h + 1) / N_ATTN_HEADS)


def _attn_tiling(L, window, dilation):
    radius = window // (2 * dilation)
    n = L // dilation
    tq = min(ATTN_Q_TILE, n)
    tk = min(tq + 2 * radius, n)
    return radius, n, tq, tk


def _attn_kernel(q_ref, k_ref, v_ref, o_ref, lse_ref, *bias_refs, L, hp_axis):
    hp = pl.program_id(hp_axis)

    def pick(values):
        out = jnp.float32(values[-1])
        for j in range(len(values) - 2, -1, -1):
            out = jnp.where(hp == j, jnp.float32(values[j]), out)
        return out

    n_pairs = ATTN_WIDTH // LANES
    slope_a = pick([_alibi_slope(2 * j) for j in range(n_pairs)])
    slope_b = pick([_alibi_slope(2 * j + 1) for j in range(n_pairs)])

    for branch, (window, dilation) in enumerate(DILATED_PATTERNS):
        radius, n, tq, tk = _attn_tiling(L, window, dilation)
        tiles = n // tq
        first = branch == 0

        lane = lax.broadcasted_iota(jnp.int32, (tq, LANES), 1)
        low = lane < HEAD_DIM
        row2 = lax.broadcasted_iota(jnp.int32, (2 * tq, tk), 0)
        col2 = lax.broadcasted_iota(jnp.int32, (2 * tq, tk), 1)
        base_rel = col2 - jnp.where(row2 >= tq, row2 - tq, row2)
        second_col = lax.broadcasted_iota(jnp.int32, (2 * tq, 1), 0) >= tq
        slope = jnp.where(second_col, slope_b, slope_a) * float(dilation)

        offsets = {q0 - min(max(q0 - radius, 0), n - tk) for q0 in range(0, n, tq)}
        assert offsets <= {0, radius, 2 * radius}
        bias_ref = bias_refs[branch]
        for case in range(3):
            absrel = jnp.abs(base_rel - case * radius)
            bias_ref[case] = jnp.where(absrel <= radius, -(slope * absrel.astype(jnp.float32)), MASK_VALUE)

        def tile(it, carry, dilation=dilation, radius=radius, n=n, tq=tq, tk=tk, tiles=tiles, first=first,
                 low=low, lane=lane, bias_ref=bias_ref):
            r = it // tiles
            q0 = (it % tiles) * tq
            ks = jnp.clip(q0 - radius, 0, n - tk)
            q_rows = pl.ds(r + dilation * q0, tq, stride=dilation)
            k_rows = pl.ds(r + dilation * ks, tk, stride=dilation)
            q = q_ref[0, 0, q_rows, :].astype(jnp.bfloat16)
            kw = k_ref[0, 0, k_rows, :].astype(jnp.bfloat16)
            vw = v_ref[0, 0, k_rows, :].astype(jnp.bfloat16)
            zero = jnp.zeros_like(q)
            q2 = jnp.concatenate([jnp.where(low, q, zero), jnp.where(low, zero, q)], axis=0)
            s = lax.dot_general(q2, kw, (((1,), (1,)), ((), ())), preferred_element_type=jnp.float32)
            s = s + bias_ref[(q0 - ks) // radius]
            m = jnp.max(s, axis=-1, keepdims=True)
            p = jnp.exp(s - m)
            l = jnp.sum(p, axis=-1, keepdims=True)
            o2 = jnp.dot(p.astype(jnp.bfloat16), vw, preferred_element_type=jnp.float32) / l
            lse2 = m + jnp.log(l)
            o = jnp.where(low, o2[:tq], o2[tq:])
            lse = jnp.where(lane == 0, lse2[:tq], jnp.where(lane == 1, lse2[tq:], 0.0))
            if not first:
                prev = lse_ref[q_rows, :]
                mx = jnp.maximum(prev, lse)
                new = mx + jnp.log(jnp.exp(prev - mx) + jnp.exp(lse - mx))
                w_prev = jnp.exp(prev - new)
                w_cur = jnp.exp(lse - new)
                o = (o_ref[0, 0, q_rows, :] * jnp.where(low, w_prev[:, 0:1], w_prev[:, 1:2])
                     + o * jnp.where(low, w_cur[:, 0:1], w_cur[:, 1:2]))
                lse = new
            return q_rows, o, lse

        per_trip = min(ATTN_TILES_PER_TRIP, dilation * tiles)

        def tiles_body(j, carry, tile=tile, per_trip=per_trip):
            done = [tile(per_trip * j + i, carry) for i in range(per_trip)]
            for q_rows, o, lse in done:
                o_ref[0, 0, q_rows, :] = o
                lse_ref[q_rows, :] = lse
            return carry

        assert (dilation * tiles) % per_trip == 0
        lax.fori_loop(0, dilation * tiles // per_trip, tiles_body, 0)


def dilated_attention(qkv):
    B, _, L, _ = qkv.shape
    n_pairs = ATTN_WIDTH // LANES
    bias_scratch = []
    for window, dilation in DILATED_PATTERNS:
        radius, n, tq, tk = _attn_tiling(L, window, dilation)
        assert L % dilation == 0 and n % tq == 0
        bias_scratch.append(pltpu.VMEM((3, 2 * tq, tk), jnp.float32))
    spec = lambda g: pl.BlockSpec((1, 1, L, LANES), lambda b, hp: (b, g * n_pairs + hp, 0, 0))
    return pl.pallas_call(
        partial(_attn_kernel, L=L, hp_axis=1),
        grid=(B, n_pairs),
        in_specs=[spec(0), spec(1), spec(2)],
        out_specs=pl.BlockSpec((1, 1, L, LANES), lambda b, hp: (b, hp, 0, 0)),
        out_shape=jax.ShapeDtypeStruct((B, n_pairs, L, LANES), jnp.float32),
        scratch_shapes=[pltpu.VMEM((L, LANES), jnp.float32)] + bias_scratch,
        compiler_params=_compiler_params(("parallel", "arbitrary")),
        name="dilated_attn",
    )(qkv, qkv, qkv)


def _out_proj_kernel(x_ref, attn_ref, hy_ref, ag_ref, hg_ref, w_ref, ng_ref, wr_ref, x1_ref, h_ref, aff_ref):
    a = ATTN_WIDTH

    def group_norm(ref, g_ref):
        v = jnp.concatenate([ref[0, s] for s in range(ref.shape[1])], axis=-1)
        return (v * _rms_scale(v) * g_ref[...]).astype(jnp.bfloat16)

    y = jnp.dot(group_norm(attn_ref, ag_ref), w_ref[:a, :], preferred_element_type=jnp.float32)
    y = y + jnp.dot(group_norm(hy_ref, hg_ref), w_ref[a:, :], preferred_element_type=jnp.float32)
    x1 = x_ref[...] + y
    x1_ref[...] = x1
    h = x1 * _rms_scale(x1) * ng_ref[...]
    h_hi = h.astype(jnp.bfloat16)
    h_ref[...] = h_hi
    h_lo = (h - h_hi.astype(jnp.float32)).astype(jnp.bfloat16)
    wr = wr_ref[...]
    wr_hi = wr.astype(jnp.bfloat16)
    wr_lo = (wr - wr_hi.astype(jnp.float32)).astype(jnp.bfloat16)
    dot = lambda p, q: jnp.dot(p, q, preferred_element_type=jnp.float32)
    logits = dot(h_hi, wr_hi) + (dot(h_lo, wr_hi) + dot(h_hi, wr_lo))
    e = jnp.exp(logits - jnp.max(logits, axis=-1, keepdims=True))
    aff_ref[...] = e / jnp.sum(e, axis=-1, keepdims=True)


def out_proj(x, attn, hy, attn_g, hy_g, w, ffn_g, w_router):
    B, L, D = x.shape
    N = B * L
    tm = min(ROW_TILE, L)
    assert L % tm == 0
    nt = L // tm
    E = w_router.shape[-1]
    slabs = lambda width: pl.BlockSpec((1, width // LANES, tm, LANES), lambda i: (i // nt, 0, i % nt, 0))
    row = lambda width: pl.BlockSpec((tm, width), lambda i: (i, 0))
    full = lambda r, c: pl.BlockSpec((r, c), lambda i: (0, 0))
    return pl.pallas_call(
        _out_proj_kernel,
        grid=(N // tm,),
        in_specs=[row(D), slabs(ATTN_WIDTH), slabs(HYENA_WIDTH), full(1, ATTN_WIDTH), full(1, HYENA_WIDTH),
                  full(D, D), full(1, D), full(D, E)],
        out_specs=[row(D), row(D), row(E)],
        out_shape=[jax.ShapeDtypeStruct((N, D), jnp.float32), jax.ShapeDtypeStruct((N, D), jnp.bfloat16),
                   jax.ShapeDtypeStruct((N, E), jnp.float32)],
        compiler_params=_compiler_params(("parallel",)),
        name="out_proj",
    )(x.reshape(N, D), attn, hy, attn_g, hy_g, w, ffn_g, w_router)


def _ffn_kernel(x_ref, gate_ref, wg_ref, wu_ref, wd_ref, o_ref, acc_ref):
    f = pl.program_id(2)
    x = x_ref[0]
    a = jnp.dot(x, wg_ref[0], preferred_element_type=jnp.float32)
    b = jnp.dot(x, wu_ref[0], preferred_element_type=jnp.float32)
    h = (a * jax.nn.sigmoid(a) * b).astype(jnp.bfloat16)
    y = jnp.dot(h, wd_ref[0], preferred_element_type=jnp.float32)

    @pl.when(f == 0)
    def _():
        acc_ref[...] = y

    @pl.when(f > 0)
    def _():
        acc_ref[...] += y

    @pl.when(f == pl.num_programs(2) - 1)
    def _():
        o_ref[0] = (acc_ref[...] * gate_ref[0]).astype(o_ref.dtype)


def expert_ffn(xe, gates, wg, wu, wd):
    E, C, D = xe.shape
    F = wg.shape[-1]
    tm = min(FFN_TOKEN_TILE, C)
    tf = min(FFN_HIDDEN_TILE, F)
    assert C % tm == 0 and F % tf == 0
    return pl.pallas_call(
        _ffn_kernel,
        grid=(E, C // tm, F // tf),
        in_specs=[
            pl.BlockSpec((1, tm, D), lambda e, i, f: (e, i, 0)),
            pl.BlockSpec((1, tm, 1), lambda e, i, f: (e, i, 0)),
            pl.BlockSpec((1, D, tf), lambda e, i, f: (e, 0, f)),
            pl.BlockSpec((1, D, tf), lambda e, i, f: (e, 0, f)),
            pl.BlockSpec((1, tf, D), lambda e, i, f: (e, f, 0)),
        ],
        out_specs=pl.BlockSpec((1, tm, D), lambda e, i, f: (e, i, 0)),
        out_shape=jax.ShapeDtypeStruct((E, C, D), jnp.bfloat16),
        scratch_shapes=[pltpu.VMEM((tm, D), jnp.float32)],
        compiler_params=_compiler_params(("parallel", "parallel", "arbitrary")),
        name="moe_ffn",
    )(xe, gates, wg, wu, wd)


ROUTE_CHUNK = LANES
ROUTE_WALKS = 8
COMBINE_TOKENS = 512
COMBINE_WINDOW = 128


def _route_select_kernel(aff_ref, tri_ref, ones_ref, low_ref, slot_ref, rank_ref, *, cap):
    a = aff_ref[0]

    def as_float(bits):
        return pltpu.bitcast(jnp.full((8, LANES), bits, jnp.int32), jnp.float32)[0:1, :]

    def bisect(_, lohi):
        lo, hi = lohi
        mid = lo + (hi - lo + 1) // 2
        ok = jnp.sum((a >= as_float(mid)).astype(jnp.int32)) >= cap
        return jnp.where(ok, mid, lo), jnp.where(ok, hi, mid - 1)

    thr_bits, _ = lax.fori_loop(0, 31, bisect, (jnp.int32(0), jnp.int32(0x7F800000)))
    thr = as_float(thr_bits)

    def exclusive_prefix(m):
        x = jnp.where(m, 1.0, 0.0).astype(jnp.bfloat16)
        incl = jnp.dot(x, tri_ref[...], preferred_element_type=jnp.float32)
        tot = jnp.dot(x, ones_ref[...], preferred_element_type=jnp.float32).astype(jnp.bfloat16)
        offs = jnp.dot(low_ref[...], tot, preferred_element_type=jnp.float32)
        return offs + incl - x.astype(jnp.float32)

    gt = a > thr
    eq = a == thr
    need = (cap - jnp.sum(gt.astype(jnp.int32))).astype(jnp.float32)
    sel = gt | (eq & (exclusive_prefix(eq) < need))
    rank = exclusive_prefix(sel).astype(jnp.int32)
    rank_ref[0] = rank
    slot_ref[0] = jnp.where(sel, rank, -1)


def route_select(aff3, cap):
    E, nc, w = aff3.shape
    tri = jnp.asarray(np.triu(np.ones((w, w), np.float32)), jnp.bfloat16)
    ones = jnp.ones((w, w), jnp.bfloat16)
    low = jnp.asarray(np.tril(np.ones((nc, nc), np.float32), -1), jnp.bfloat16)
    blk = pl.BlockSpec((1, nc, w), lambda e: (e, 0, 0))
    full = lambda a: pl.BlockSpec(a.shape, lambda e: (0,) * a.ndim)
    out = jax.ShapeDtypeStruct((E, nc, w), jnp.int32)
    return pl.pallas_call(
        partial(_route_select_kernel, cap=cap),
        grid=(E,),
        in_specs=[blk, full(tri), full(ones), full(low)],
        out_specs=[blk, blk],
        out_shape=[out, out],
        compiler_params=_compiler_params(("parallel",)),
        name="route_select",
    )(aff3, tri, ones, low)


def _route_compact_kernel(offs_ref, slot_ref, aff_ref, idx_ref, gate_ref, acc_ref, *, cap):
    e = pl.program_id(0)
    w = ROUTE_CHUNK
    nc = slot_ref.shape[1]
    span = nc // ROUTE_WALKS
    last_block = (cap - 1) // w
    sub = lax.broadcasted_iota(jnp.int32, (2 * w, w), 0)
    lane = lax.broadcasted_iota(jnp.int32, (1, w), 1)
    zero_row = jnp.zeros((1, w), jnp.float32)

    def block_of(c):
        return jnp.where(c < nc, jnp.minimum(offs_ref[e, jnp.minimum(c, nc - 1)] // w, last_block), last_block)

    def store(walk, block, rows):
        acc_ref[walk, 0, pl.ds(block, 1), :] = rows[0:1] * 256.0 + rows[1:2]
        acc_ref[walk, 1, pl.ds(block, 1), :] = rows[2:3] + rows[3:4] + rows[4:5]

    def chunk(walk, c, carry):
        b0 = block_of(c)
        rel = slot_ref[0, pl.ds(c, 1), :] - b0 * w
        onehot = jnp.where(sub == rel, 1.0, 0.0).astype(jnp.bfloat16)
        tok = c * w + lane
        g = aff_ref[0, pl.ds(c, 1), :]
        g_hi = g.astype(jnp.bfloat16).astype(jnp.float32)
        g_mid = (g - g_hi).astype(jnp.bfloat16).astype(jnp.float32)
        g_lo = g - g_hi - g_mid
        vals = jnp.concatenate([(tok >> 8).astype(jnp.float32), (tok & 255).astype(jnp.float32), g_hi, g_mid,
                                g_lo, zero_row, zero_row, zero_row], axis=0).astype(jnp.bfloat16)
        res = lax.dot_general(vals, onehot, (((1,), (1,)), ((), ())), preferred_element_type=jnp.float32)
        total = carry + res[:, :w]
        store(walk, b0, total)
        return jnp.where(block_of(c + 1) == b0, total, res[:, w:])

    acc_ref[...] = jnp.zeros_like(acc_ref)

    def step(j, carries):
        return tuple(chunk(q, q * span + j, carries[q]) for q in range(ROUTE_WALKS))

    zeros = jnp.zeros((8, w), jnp.float32)
    carries = lax.fori_loop(0, span, step, (zeros,) * ROUTE_WALKS, unroll=2)
    for q in range(ROUTE_WALKS):
        store(q, block_of((q + 1) * span), carries[q])
    idx_ref[0] = sum(acc_ref[q, 0] for q in range(ROUTE_WALKS)).astype(jnp.int32)
    gate_ref[0] = sum(acc_ref[q, 1] for q in range(ROUTE_WALKS))


def route_compact(slot3, aff3, offs, cap):
    E, nc, w = slot3.shape
    assert cap % w == 0 and nc % ROUTE_WALKS == 0
    blk = pl.BlockSpec((1, nc, w), lambda e, offs: (e, 0, 0))
    out_blk = pl.BlockSpec((1, cap // w, w), lambda e, offs: (e, 0, 0))
    idx, gates = pl.pallas_call(
        partial(_route_compact_kernel, cap=cap),
        grid_spec=pltpu.PrefetchScalarGridSpec(
            num_scalar_prefetch=1, grid=(E,), in_specs=[blk, blk], out_specs=[out_blk, out_blk],
            scratch_shapes=[pltpu.VMEM((ROUTE_WALKS, 2, cap // w, w), jnp.float32)]),
        out_shape=[jax.ShapeDtypeStruct((E, cap // w, w), jnp.int32),
                   jax.ShapeDtypeStruct((E, cap // w, w), jnp.float32)],
        compiler_params=_compiler_params(("arbitrary",)),
        name="route_compact",
    )(offs, slot3, aff3)
    return idx.reshape(E, cap), gates.reshape(E, cap)


def _combine_kernel(starts_ref, x_ref, slot_ref, g_ref, ye_hbm, o_ref, stage_ref, y_ref, sems, *, cap):
    j = pl.program_id(0)
    n_tiles = pl.num_programs(0)
    n_experts = slot_ref.shape[1]
    t = x_ref.shape[0]
    win = COMBINE_WINDOW
    buf = j % 2

    def window_copy(b, e, start):
        return pltpu.make_async_copy(ye_hbm.at[e, pl.ds(start, win), :], stage_ref.at[b, pl.ds(e * win, win), :],
                                     sems.at[b, e])

    def aligned(start):
        return pl.multiple_of(jnp.minimum((start // 16) * 16, cap - win), 16)

    def first_windows(tile):
        return [aligned(starts_ref[e, tile]) for e in range(n_experts)]

    @pl.when(j == 0)
    def _():
        for e, start in enumerate(first_windows(0)):
            window_copy(0, e, start).start()

    @pl.when(j + 1 < n_tiles)
    def _():
        for e, start in enumerate(first_windows(j + 1)):
            window_copy(1 - buf, e, start).start()

    bases = first_windows(j)
    lane = lax.broadcasted_iota(jnp.int32, (t, win), 1)
    slot = slot_ref[...]
    onehot = jnp.concatenate(
        [jnp.where(slot[:, e:e + 1] - bases[e] == lane, 1.0, 0.0).astype(jnp.bfloat16) for e in range(n_experts)],
        axis=1)
    for e in range(n_experts):
        window_copy(buf, e, bases[e]).wait()
    y_ref[...] = jnp.dot(onehot, stage_ref[buf], preferred_element_type=jnp.float32)

    for e in range(n_experts):
        end = starts_ref[e, j + 1]
        n_more = jnp.maximum(end - bases[e] + win - 1, 0) // win - 1

        def more(k, carry, e=e):
            first = bases[e] + (k + 1) * win
            start = aligned(first)
            cp = window_copy(buf, e, start)
            cp.start()
            cp.wait()
            col = slot_ref[:, e:e + 1]
            hit = (col - start == lane) & (col >= first)
            rows = stage_ref[buf, pl.ds(e * win, win), :]
            y_ref[...] += jnp.dot(jnp.where(hit, 1.0, 0.0).astype(jnp.bfloat16), rows,
                                  preferred_element_type=jnp.float32)
            return carry

        lax.fori_loop(0, n_more, more, 0)

    x = x_ref[...] + y_ref[...]
    o_ref[...] = x * _rms_scale(x) * g_ref[...]


def combine(x1, slot_t, starts, ye, g):
    N, D = x1.shape
    E, cap, _ = ye.shape
    t = min(COMBINE_TOKENS, N)
    assert N % t == 0 and cap % COMBINE_WINDOW == 0
    return pl.pallas_call(
        partial(_combine_kernel, cap=cap),
        grid_spec=pltpu.PrefetchScalarGridSpec(
            num_scalar_prefetch=1, grid=(N // t,),
            in_specs=[pl.BlockSpec((t, D), lambda j, s: (j, 0)), pl.BlockSpec((t, E), lambda j, s: (j, 0)),
                      pl.BlockSpec((1, D), lambda j, s: (0, 0)), pl.BlockSpec(memory_space=pl.ANY)],
            out_specs=pl.BlockSpec((t, D), lambda j, s: (j, 0)),
            scratch_shapes=[pltpu.VMEM((2, E * COMBINE_WINDOW, D), jnp.bfloat16), pltpu.VMEM((t, D), jnp.float32),
                            pltpu.SemaphoreType.DMA((2, E))]),
        out_shape=jax.ShapeDtypeStruct((N, D), jnp.float32),
        compiler_params=_compiler_params(("arbitrary",)),
        name="moe_combine",
    )(starts, x1, slot_t, g, ye)


DFT_P = 128
SLAB_PAD = 8
X_PITCH = DFT_P + SLAB_PAD
A_PITCH = 2 * DFT_P + SLAB_PAD
HY_SLABS = HYENA_WIDTH // LANES
U_SLABS = 3 * HY_SLABS
SLOW_STAGE_UNROLL = 32
SPECTRAL_UNROLL = 11


def _dft_sizes(M):
    q = M // DFT_P
    ka = q // 2 + 1
    kp = -(-ka // 8) * 8
    return q, ka, kp


def _dft_constants(M):
    p = DFT_P
    q, ka, kp = _dft_sizes(M)
    k = np.arange(ka)

    def slow_fwd(n_ts):
        ang = 2.0 * np.pi * ((k[:, None] * np.arange(n_ts)[None, :]) % q) / q
        m = np.zeros((2 * kp, n_ts))
        m[:ka] = np.cos(ang)
        m[kp:kp + ka] = -np.sin(ang)
        return m

    ts = np.arange(q // 2)
    ang = 2.0 * np.pi * ((ts[:, None] * k[None, :]) % q) / q
    c = np.where((k == 0) | (k == q // 2), 1.0, 2.0)[None, :]
    slow_inv = np.zeros((q // 2, 2 * kp))
    slow_inv[:, :ka] = c * np.cos(ang) / M
    slow_inv[:, kp:kp + ka] = -c * np.sin(ang) / M

    kb = np.arange(p)
    tf = np.arange(p)
    idx = (tf[None, None, :] * (k[:, None, None] + q * kb[None, :, None])) % M
    wr = np.cos(2.0 * np.pi * idx / M)
    wi = -np.sin(2.0 * np.pi * idx / M)
    w2 = np.concatenate([np.concatenate([wr, -wi], axis=2), np.concatenate([wi, wr], axis=2)], axis=1)
    bf = lambda a: jnp.asarray(a, jnp.float32).astype(jnp.bfloat16)
    return {"fa_half": bf(slow_fwd(q // 2)), "fa_full": bf(slow_fwd(q)), "fi": bf(slow_inv),
            "w2": bf(w2), "v2": bf(np.transpose(w2, (0, 2, 1)))}


def _slow_stage_forward(xs_ref, ag_ref, fa_ref, n_ts, kp):
    fa = fa_ref[...]

    def body(tf, carry):
        xt = xs_ref[pl.ds(tf, n_ts, stride=X_PITCH), :].astype(jnp.bfloat16)
        a = jnp.dot(fa, xt, preferred_element_type=jnp.float32)
        ag_ref[pl.ds(tf, kp, stride=A_PITCH), :] = a[:kp]
        ag_ref[pl.ds(DFT_P + tf, kp, stride=A_PITCH), :] = a[kp:]
        return carry

    lax.fori_loop(0, DFT_P, body, 0, unroll=SLOW_STAGE_UNROLL)


def _slow_stage_inverse(ag_ref, ys_ref, fi_ref, n_ts, kp):
    fi = fi_ref[...]

    def body(tf, carry):
        gr = ag_ref[pl.ds(tf, kp, stride=A_PITCH), :]
        gi = ag_ref[pl.ds(DFT_P + tf, kp, stride=A_PITCH), :]
        g = jnp.concatenate([gr, gi], axis=0).astype(jnp.bfloat16)
        ys_ref[pl.ds(tf, n_ts, stride=X_PITCH), :] = jnp.dot(fi, g, preferred_element_type=jnp.float32)
        return carry

    lax.fori_loop(0, DFT_P, body, 0, unroll=SLOW_STAGE_UNROLL)


def _fast_stage(ag_ref, w2_ref, k):
    r0 = pl.multiple_of(k * A_PITCH, 8)
    a2 = ag_ref[pl.ds(r0, 2 * DFT_P), :].astype(jnp.bfloat16)
    return r0, jnp.dot(w2_ref[k], a2, preferred_element_type=jnp.float32)


def _spectral_product(ag_ref, w2_ref, v2_ref, kspec_ref, ka):
    p = DFT_P

    def block(k):
        r0, x2 = _fast_stage(ag_ref, w2_ref, k)
        kk = kspec_ref[0, 0, pl.ds(pl.multiple_of(k * 2 * p, 2 * p), 2 * p), :]
        xr, xi, kr, ki = x2[:p], x2[p:], kk[:p], kk[p:]
        y2 = jnp.concatenate([xr * kr - xi * ki, xr * ki + xi * kr], axis=0).astype(jnp.bfloat16)
        return r0, jnp.dot(v2_ref[k], y2, preferred_element_type=jnp.float32)

    def body(j, carry):
        done = [block(SPECTRAL_UNROLL * j + i) for i in range(SPECTRAL_UNROLL)]
        for r0, g2 in done:
            ag_ref[pl.ds(r0, 2 * p), :] = g2
        return carry

    lax.fori_loop(0, ka // SPECTRAL_UNROLL, body, 0)
    rest = [block(k) for k in range(ka - ka % SPECTRAL_UNROLL, ka)]
    for r0, g2 in rest:
        ag_ref[pl.ds(r0, 2 * p), :] = g2


def _hyena_kernel(uv_ref, u1_ref, u2_ref, sw_ref, sb_ref, skip_ref, k0_ref, k1_ref, fa_ref, fi_ref, w2_ref,
                  v2_ref, o_ref, xs_ref, ys_ref, ag_ref, *, L):
    p = DFT_P
    c = pl.program_id(0)
    n_ts = L // p
    _, ka, kp = _dft_sizes(2 * L)
    row = lax.broadcasted_iota(jnp.int32, (p, LANES), 0)

    def short_conv_block(u_ref, group, ts):
        r0 = pl.multiple_of(ts * p, p)
        mid = u_ref[0, 0, pl.ds(r0, p), :]
        prev8 = u_ref[0, 0, pl.ds(pl.multiple_of(jnp.maximum(r0 - 8, 0), 8), 8), :]
        next8 = u_ref[0, 0, pl.ds(pl.multiple_of(jnp.minimum(r0 + p, L - 8), 8), 8), :]
        has_prev = (r0 > 0).astype(jnp.float32)
        has_next = (r0 + p < L).astype(jnp.float32)
        up = jnp.where(row == 0, prev8[7:8, :] * has_prev, pltpu.roll(mid, 1, 0))
        dn = jnp.where(row == p - 1, next8[0:1, :] * has_next, pltpu.roll(mid, p - 1, 0))
        ch = group * HY_SLABS + c
        w = lambda tap: sw_ref[pl.ds(tap * U_SLABS + ch, 1), :]
        return up * w(0) + mid * w(1) + dn * w(2) + sb_ref[pl.ds(ch, 1), :]

    def long_conv(kspec_ref):
        _slow_stage_forward(xs_ref, ag_ref, fa_ref, n_ts, kp)
        _spectral_product(ag_ref, w2_ref, v2_ref, kspec_ref, ka)
        _slow_stage_inverse(ag_ref, ys_ref, fi_ref, n_ts, kp)

    def blocks(fn):
        def body(ts, carry):
            fn(ts, pl.ds(pl.multiple_of(ts * X_PITCH, 8), p))
            return carry
        lax.fori_loop(0, n_ts, body, 0)

    def stage_v(ts, rows):
        xs_ref[rows, :] = short_conv_block(uv_ref, 0, ts)

    def stage_z(ts, rows):
        v = xs_ref[rows, :]
        y1 = ys_ref[rows, :] + skip_ref[pl.ds(c, 1), :] * v
        xs_ref[rows, :] = short_conv_block(u1_ref, 1, ts) * y1

    def stage_out(ts, rows):
        z = xs_ref[rows, :]
        y2 = ys_ref[rows, :] + skip_ref[pl.ds(HY_SLABS + c, 1), :] * z
        o_ref[0, 0, pl.ds(pl.multiple_of(ts * p, p), p), :] = short_conv_block(u2_ref, 2, ts) * y2

    blocks(stage_v)
    long_conv(k0_ref)
    blocks(stage_z)
    long_conv(k1_ref)
    blocks(stage_out)


def hyena_mix(u, sw, sb, skip, kspec, consts):
    B, n_slabs, L, _ = u.shape
    q, ka, kp = _dft_sizes(2 * L)
    n_ts = L // DFT_P
    u_spec = lambda g: pl.BlockSpec((1, 1, L, LANES), lambda c, b: (b, g * HY_SLABS + c, 0, 0))
    k_spec = lambda o: pl.BlockSpec((1, 1, ka * 2 * DFT_P, LANES), lambda c, b: (o, c, 0, 0),
                                    pipeline_mode=pl.Buffered(1))
    const = lambda a: pl.BlockSpec(a.shape, lambda c, b: (0,) * a.ndim, pipeline_mode=pl.Buffered(1))
    small = lambda a: pl.BlockSpec(a.shape, lambda c, b: (0,) * a.ndim)
    fa, fi, w2, v2 = consts["fa_half"], consts["fi"], consts["w2"], consts["v2"]
    return pl.pallas_call(
        partial(_hyena_kernel, L=L),
        grid=(HY_SLABS, B),
        in_specs=[u_spec(0), u_spec(1), u_spec(2), small(sw), small(sb), small(skip), k_spec(0), k_spec(1),
                  small(fa), small(fi), const(w2), const(v2)],
        out_specs=pl.BlockSpec((1, 1, L, LANES), lambda c, b: (b, c, 0, 0)),
        out_shape=jax.ShapeDtypeStruct((B, HY_SLABS, L, LANES), jnp.float32),
        scratch_shapes=[pltpu.VMEM((n_ts * X_PITCH, LANES), jnp.float32),
                        pltpu.VMEM((n_ts * X_PITCH, LANES), jnp.float32),
                        pltpu.VMEM((kp * A_PITCH, LANES), jnp.float32)],
        compiler_params=_compiler_params(("arbitrary", "arbitrary")),
        name="hyena_mix",
    )(u, u, u, sw, sb, skip, kspec, kspec, fa, fi, w2, v2)


def _filter_spectrum_kernel(kern_ref, fa_ref, w2_ref, o_ref, xs_ref, ag_ref, *, M):
    p = DFT_P
    q, ka, kp = _dft_sizes(M)

    def load(ts, carry):
        xs_ref[pl.ds(pl.multiple_of(ts * X_PITCH, 8), p), :] = kern_ref[0, 0, pl.ds(pl.multiple_of(ts * p, p), p), :]
        return carry

    lax.fori_loop(0, q, load, 0)
    _slow_stage_forward(xs_ref, ag_ref, fa_ref, q, kp)

    def body(k, carry):
        _, x2 = _fast_stage(ag_ref, w2_ref, k)
        o_ref[0, 0, pl.ds(pl.multiple_of(k * 2 * p, 2 * p), 2 * p), :] = x2
        return carry

    lax.fori_loop(0, ka, body, 0)


def filter_spectrum(kern, consts):
    n_o, n_c, M, _ = kern.shape
    q, ka, kp = _dft_sizes(M)
    fa, w2 = consts["fa_full"], consts["w2"]
    full = lambda a: pl.BlockSpec(a.shape, lambda o, c: (0,) * a.ndim)
    return pl.pallas_call(
        partial(_filter_spectrum_kernel, M=M),
        grid=(n_o, n_c),
        in_specs=[pl.BlockSpec((1, 1, M, LANES), lambda o, c: (o, c, 0, 0)), full(fa), full(w2)],
        out_specs=pl.BlockSpec((1, 1, ka * 2 * DFT_P, LANES), lambda o, c: (o, c, 0, 0)),
        out_shape=jax.ShapeDtypeStruct((n_o, n_c, ka * 2 * DFT_P, LANES), jnp.float32),
        scratch_shapes=[pltpu.VMEM((q * X_PITCH, LANES), jnp.float32),
                        pltpu.VMEM((kp * A_PITCH, LANES), jnp.float32)],
        compiler_params=_compiler_params(("arbitrary", "arbitrary")),
        name="filter_spectrum",
    )(kern, fa, w2)


FILTER_ROW_TILE = 512


def _filter_kernel(bands_ref, w1t_ref, w1c_ref, w1s_ref, b1_ref, w2_ref, b2_ref, w3_ref, b3_ref, w4_ref, freq_ref,
                   absd_ref, o_ref, *, L):
    tr = o_ref.shape[2]
    hp = lax.Precision.HIGHEST
    rho = pl.program_id(0) * tr + lax.broadcasted_iota(jnp.int32, (tr, 1), 0)
    tau = jnp.where(rho < L, rho, 2 * L - rho).astype(jnp.float32)
    t = tau * (1.0 / (L - 1))
    ang = (2.0 * math.pi * tau / L) * bands_ref[...]
    dot = lambda a, b: jnp.dot(a, b, preferred_element_type=jnp.float32, precision=hp)
    fr = freq_ref[...]
    pre = t * w1t_ref[...] + dot(jnp.cos(ang), w1c_ref[...]) - dot(jnp.sin(ang), w1s_ref[...]) + b1_ref[...]
    h = jnp.sin(fr * pre)
    h = jnp.sin(fr * (dot(h, w2_ref[...]) + b2_ref[...]))
    h = jnp.sin(fr * (dot(h, w3_ref[...]) + b3_ref[...]))
    hh = dot(h, w4_ref[...])
    decay = jnp.exp(-t * absd_ref[...])
    fwd = rho < L
    nonzero = rho != L
    w = HYENA_WIDTH
    for o in range(HYENA_ORDER):
        for c in range(HY_SLABS):
            f = hh[:, (2 * o) * w + c * LANES:(2 * o) * w + (c + 1) * LANES]
            b = hh[:, (2 * o + 1) * w + c * LANES:(2 * o + 1) * w + (c + 1) * LANES]
            val = jnp.where(fwd, f, b) * decay[:, c * LANES:(c + 1) * LANES]
            o_ref[o, c] = jnp.where(nonzero, val, 0.0)


def hyena_kernels(L, w1, b1, w2, b2, w3, b3, w4, freq):
    M = 2 * L
    tr = min(FILTER_ROW_TILE, M)
    bands = np.zeros((1, LANES), np.float32)
    bands[0, :FILTER_BANDS] = np.linspace(1e-4, FILTER_BANDS - 1, FILTER_BANDS, dtype=np.float32)
    pad_rows = lambda a: jnp.zeros((LANES, a.shape[1]), jnp.float32).at[:a.shape[0]].set(a)
    min_decay = math.log(DECAY_TARGET) / FAST_DECAY_PCT
    max_decay = math.log(DECAY_TARGET) / SLOW_DECAY_PCT
    absd = np.abs(np.linspace(min_decay, max_decay, HYENA_WIDTH, dtype=np.float32))[None, :]
    args = [jnp.asarray(bands), w1[0:1], pad_rows(w1[1:1 + FILTER_BANDS]), pad_rows(w1[1 + FILTER_BANDS:]),
            b1[None, :], w2, b2[None, :], w3, b3[None, :], w4, freq[None, :], jnp.asarray(absd)]
    full = lambda a: pl.BlockSpec(a.shape, lambda i: (0,) * a.ndim)
    return pl.pallas_call(
        partial(_filter_kernel, L=L),
        grid=(M // tr,),
        in_specs=[full(a) for a in args],
        out_specs=pl.BlockSpec((HYENA_ORDER, HY_SLABS, tr, LANES), lambda i: (0, 0, i, 0)),
        out_shape=jax.ShapeDtypeStruct((HYENA_ORDER, HY_SLABS, M, LANES), jnp.float32),
        compiler_params=_compiler_params(("parallel",)),
        name="hyena_filter",
    )(*args)


def hyena_operator(u, p):
    L = u.shape[2]
    consts = _dft_constants(2 * L)
    kern = hyena_kernels(L, p["filt_w1"], p["filt_b1"], p["filt_w2"], p["filt_b2"], p["filt_w3"], p["filt_b3"],
                         p["filt_w4"], p["filt_freq"])
    kspec = filter_spectrum(kern, consts)
    sw = p["short_w"].reshape(SHORT_CONV * U_SLABS, LANES)
    sb = p["short_b"].reshape(U_SLABS, LANES)
    skip = p["hyena_skip"].reshape(HYENA_ORDER * HY_SLABS, LANES)
    return hyena_mix(u, sw, sb, skip, kspec, consts)


def expert_choice_ffn(x1, h, affinity, p):
    N, D = h.shape
    E = affinity.shape[1]
    cap = EC_CAPACITY * N // N_EXPERTS
    aff3 = affinity.T.reshape(E, N // ROUTE_CHUNK, ROUTE_CHUNK)
    slot3, rank3 = route_select(aff3, cap)
    idx, gates = route_compact(slot3, aff3, rank3[:, :, 0], cap)
    ye = expert_ffn(h[idx], gates[..., None], p["w_gate"], p["w_up"], p["w_down"])
    t = min(COMBINE_TOKENS, N)
    starts = jnp.concatenate([rank3.reshape(E, N)[:, ::t], jnp.full((E, 1), cap, jnp.int32)], axis=1)
    return combine(x1, slot3.reshape(E, N).T, starts, ye, p["final_g"])


def encoder_trunk(x, p):
    B, L, D = x.shape
    qkv, u = in_proj(x, p["norm_mix_g"], p["w_in"])
    attn = dilated_attention(qkv)
    hy = hyena_operator(u, p)
    x1, h, affinity = out_proj(x, attn, hy, p["attn_out_g"], p["hyena_out_g"], p["w_out"], p["norm_ffn_g"],
                               p["w_router"])
    return expert_choice_ffn(x1, h, affinity, p).reshape(B, L, D)


def kernel(x_prompt, x_sample, norm_mix_g, w_in, short_w, short_b, filt_w1, filt_b1, filt_w2, filt_b2, filt_w3, filt_b3, filt_w4, filt_freq, hyena_skip, attn_out_g, hyena_out_g, w_out, norm_ffn_g, w_router, w_gate, w_up, w_down, final_g):
    bf16 = jnp.bfloat16
    p = {
        "norm_mix_g": norm_mix_g, "w_in": w_in[0].astype(bf16),
        "short_w": short_w[0], "short_b": short_b[0],
        "filt_w1": filt_w1[0], "filt_b1": filt_b1[0], "filt_w2": filt_w2[0], "filt_b2": filt_b2[0],
        "filt_w3": filt_w3[0], "filt_b3": filt_b3[0], "filt_w4": filt_w4[0], "filt_freq": filt_freq[0],
        "hyena_skip": hyena_skip[0], "attn_out_g": attn_out_g, "hyena_out_g": hyena_out_g,
        "w_out": w_out[0].astype(bf16), "norm_ffn_g": norm_ffn_g, "w_router": w_router[0],
        "w_gate": w_gate[0].astype(bf16), "w_up": w_up[0].astype(bf16), "w_down": w_down[0].astype(bf16),
        "final_g": final_g.reshape(1, D_MODEL),
    }
    return (encoder_trunk(x_prompt, p), encoder_trunk(x_sample, p))
```

```python
import math
from functools import partial

import jax
import jax.numpy as jnp
import numpy as np
from jax import lax
from jax.experimental import pallas as pl
from jax.experimental.pallas import tpu as pltpu

D_MODEL = 1024
HEAD_DIM = 64
N_ATTN_HEADS = 8
ATTN_WIDTH = N_ATTN_HEADS * HEAD_DIM
HYENA_WIDTH = D_MODEL - ATTN_WIDTH
HYENA_ORDER = 2
SHORT_CONV = 3
FILTER_EMB = 33
FILTER_BANDS = (FILTER_EMB - 1) // 2
FAST_DECAY_PCT = 0.3
SLOW_DECAY_PCT = 1.5
DECAY_TARGET = 1e-2
DILATED_PATTERNS = ((128, 1), (512, 4), (2048, 16))
N_EXPERTS = 16
EC_CAPACITY = 2
D_EXPERT = 2816
NORM_EPS = 1e-6
MASK_VALUE = -1e30

LANES = 128
VMEM_LIMIT_BYTES = 56 * 1024 * 1024

ROW_TILE = 512
FFN_TOKEN_TILE = 1024
FFN_HIDDEN_TILE = D_EXPERT // 2
ATTN_Q_TILE = 128
ATTN_TILES_PER_TRIP = 8


def _compiler_params(semantics):
    return pltpu.CompilerParams(dimension_semantics=semantics, vmem_limit_bytes=VMEM_LIMIT_BYTES)


def _rms_scale(x):
    return lax.rsqrt(jnp.mean(x * x, axis=-1, keepdims=True) + NORM_EPS)


def _in_proj_kernel(x_ref, g_ref, w_ref, qkv_ref, u_ref):
    x = x_ref[...]
    h = (x * _rms_scale(x) * g_ref[...]).astype(jnp.bfloat16)
    a = ATTN_WIDTH
    qkv = jnp.dot(h, w_ref[:, :3 * a], preferred_element_type=jnp.float32)
    for s in range(qkv_ref.shape[1]):
        cols = qkv[:, s * LANES:(s + 1) * LANES]
        qkv_ref[0, s] = cols * (HEAD_DIM ** -0.5) if s < a // LANES else cols
    u = jnp.dot(h, w_ref[:, 3 * a:], preferred_element_type=jnp.float32)
    for s in range(u_ref.shape[1]):
        u_ref[0, s] = u[:, s * LANES:(s + 1) * LANES]


def in_proj(x, g, w):
    B, L, D = x.shape
    N = B * L
    tm = min(ROW_TILE, L)
    assert L % tm == 0
    nt = L // tm
    a3, h3 = 3 * ATTN_WIDTH, 3 * HYENA_WIDTH
    slabs = lambda width: pl.BlockSpec((1, width // LANES, tm, LANES), lambda i: (i // nt, 0, i % nt, 0))
    return pl.pallas_call(
        _in_proj_kernel,
        grid=(N // tm,),
        in_specs=[
            pl.BlockSpec((tm, D), lambda i: (i, 0)),
            pl.BlockSpec((1, D), lambda i: (0, 0)),
            pl.BlockSpec((D, a3 + h3), lambda i: (0, 0)),
        ],
        out_specs=[slabs(a3), slabs(h3)],
        out_shape=[jax.ShapeDtypeStruct((B, a3 // LANES, L, LANES), jnp.float32),
                   jax.ShapeDtypeStruct((B, h3 // LANES, L, LANES), jnp.float32)],
        compiler_params=_compiler_params(("parallel",)),
        name="in_proj",
    )(x.reshape(N, D), g, w)


def _alibi_slope(h):
    return 2.0 ** (-8.0 * (h + 1) / N_ATTN_HEADS)


def _attn_tiling(L, window, dilation):
    radius = window // (2 * dilation)
    n = L // dilation
    tq = min(ATTN_Q_TILE, n)
    tk = min(tq + 2 * radius, n)
    return radius, n, tq, tk


def _attn_kernel(q_ref, k_ref, v_ref, o_ref, ob_ref, lse_ref, *bias_refs, L, hp_axis):
    hp = pl.program_id(hp_axis)

    def pick(values):
        out = jnp.float32(values[-1])
        for j in range(len(values) - 2, -1, -1):
            out = jnp.where(hp == j, jnp.float32(values[j]), out)
        return out

    n_pairs = ATTN_WIDTH // LANES
    slope_a = pick([_alibi_slope(2 * j) for j in range(n_pairs)])
    slope_b = pick([_alibi_slope(2 * j + 1) for j in range(n_pairs)])

    for branch, (window, dilation) in enumerate(DILATED_PATTERNS):
        radius, n, tq, tk = _attn_tiling(L, window, dilation)
        tiles = n // tq

        lane = lax.broadcasted_iota(jnp.int32, (tq, LANES), 1)
        low = lane < HEAD_DIM
        row2 = lax.broadcasted_iota(jnp.int32, (2 * tq, tk), 0)
        col2 = lax.broadcasted_iota(jnp.int32, (2 * tq, tk), 1)
        base_rel = col2 - jnp.where(row2 >= tq, row2 - tq, row2)
        second_col = lax.broadcasted_iota(jnp.int32, (2 * tq, 1), 0) >= tq
        slope = jnp.where(second_col, slope_b, slope_a) * float(dilation)

        offsets = {q0 - min(max(q0 - radius, 0), n - tk) for q0 in range(0, n, tq)}
        assert offsets <= {0, radius, 2 * radius}
        bias_ref = bias_refs[branch]
        for case in range(3):
            absrel = jnp.abs(base_rel - case * radius)
            bias_ref[case] = jnp.where(absrel <= radius, -(slope * absrel.astype(jnp.float32)), MASK_VALUE)

        def tile(it, carry, dilation=dilation, radius=radius, n=n, tq=tq, tk=tk, tiles=tiles,
                 low=low, lane=lane, bias_ref=bias_ref):
            r = it // tiles
            q0 = (it % tiles) * tq
            ks = jnp.clip(q0 - radius, 0, n - tk)
            q_rows = pl.ds(r + dilation * q0, tq, stride=dilation)
            k_rows = pl.ds(r + dilation * ks, tk, stride=dilation)
            q = q_ref[0, 0, q_rows, :].astype(jnp.bfloat16)
            kw = k_ref[0, 0, k_rows, :].astype(jnp.bfloat16)
            vw = v_ref[0, 0, k_rows, :].astype(jnp.bfloat16)
            zero = jnp.zeros_like(q)
            q2 = jnp.concatenate([jnp.where(low, q, zero), jnp.where(low, zero, q)], axis=0)
            s = lax.dot_general(q2, kw, (((1,), (1,)), ((), ())), preferred_element_type=jnp.float32)
            s = s + bias_ref[(q0 - ks) // radius]
            m = jnp.max(s, axis=-1, keepdims=True)
            p = jnp.exp(s - m)
            l = jnp.sum(p, axis=-1, keepdims=True)
            o2 = jnp.dot(p.astype(jnp.bfloat16), vw, preferred_element_type=jnp.float32) / l
            lse2 = m + jnp.log(l)
            o = jnp.where(low, o2[:tq], o2[tq:])
            lse = jnp.where(lane == 0, lse2[:tq], jnp.where(lane == 1, lse2[tq:], 0.0))
            return q_rows, o, lse

        per_trip = min(ATTN_TILES_PER_TRIP, dilation * tiles)

        def tiles_body(j, carry, tile=tile, per_trip=per_trip, branch=branch):
            done = [tile(per_trip * j + i, carry) for i in range(per_trip)]
            for q_rows, o, lse in done:
                ob_ref[branch, q_rows, :] = o
                lse_ref[branch, q_rows, :] = lse
            return carry

        assert (dilation * tiles) % per_trip == 0
        lax.fori_loop(0, dilation * tiles // per_trip, tiles_body, 0)

    n_branches = len(DILATED_PATTERNS)
    tm = min(ATTN_Q_TILE, L)
    src = lax.broadcasted_iota(jnp.int32, (LANES, LANES), 0)
    dst = lax.broadcasted_iota(jnp.int32, (LANES, LANES), 1)
    spread = jnp.where(src * HEAD_DIM <= dst, jnp.where(dst < (src + 1) * HEAD_DIM, 1.0, 0.0), 0.0)
    spread = spread.astype(jnp.bfloat16)

    def mix(i, carry):
        rows = pl.ds(pl.multiple_of(i * tm, tm), tm)
        lses = [lse_ref[b, rows, :] for b in range(n_branches)]
        mx = lses[0]
        for x in lses[1:]:
            mx = jnp.maximum(mx, x)
        es = [jnp.exp(x - mx) for x in lses]
        inv = 1.0 / sum(es)
        w = jnp.concatenate([e * inv for e in es], axis=0)
        w_hi = w.astype(jnp.bfloat16)
        w_lo = (w - w_hi.astype(jnp.float32)).astype(jnp.bfloat16)
        wide = (jnp.dot(w_hi, spread, preferred_element_type=jnp.float32)
                + jnp.dot(w_lo, spread, preferred_element_type=jnp.float32))
        o = 0.0
        for b in range(n_branches):
            o = o + ob_ref[b, rows, :] * wide[b * tm:(b + 1) * tm]
        o_ref[0, 0, rows, :] = o
        return carry

    lax.fori_loop(0, L // tm, mix, 0, unroll=4)


def dilated_attention(qkv):
    B, _, L, _ = qkv.shape
    n_pairs = ATTN_WIDTH // LANES
    bias_scratch = []
    for window, dilation in DILATED_PATTERNS:
        radius, n, tq, tk = _attn_tiling(L, window, dilation)
        assert L % dilation == 0 and n % tq == 0
        bias_scratch.append(pltpu.VMEM((3, 2 * tq, tk), jnp.float32))
    branch_state = pltpu.VMEM((len(DILATED_PATTERNS), L, LANES), jnp.float32)
    spec = lambda g: pl.BlockSpec((1, 1, L, LANES), lambda b, hp: (b, g * n_pairs + hp, 0, 0))
    return pl.pallas_call(
        partial(_attn_kernel, L=L, hp_axis=1),
        grid=(B, n_pairs),
        in_specs=[spec(0), spec(1), spec(2)],
        out_specs=pl.BlockSpec((1, 1, L, LANES), lambda b, hp: (b, hp, 0, 0)),
        out_shape=jax.ShapeDtypeStruct((B, n_pairs, L, LANES), jnp.float32),
        scratch_shapes=[branch_state, branch_state] + bias_scratch,
        compiler_params=_compiler_params(("parallel", "arbitrary")),
        name="dilated_attn",
    )(qkv, qkv, qkv)


def _out_proj_kernel(x_ref, attn_ref, hy_ref, ag_ref, hg_ref, w_ref, ng_ref, wr_ref, x1_ref, h_ref, aff_ref):
    a = ATTN_WIDTH

    def group_norm(ref, g_ref):
        v = jnp.concatenate([ref[0, s] for s in range(ref.shape[1])], axis=-1)
        return (v * _rms_scale(v) * g_ref[...]).astype(jnp.bfloat16)

    y = jnp.dot(group_norm(attn_ref, ag_ref), w_ref[:a, :], preferred_element_type=jnp.float32)
    y = y + jnp.dot(group_norm(hy_ref, hg_ref), w_ref[a:, :], preferred_element_type=jnp.float32)
    x1 = x_ref[...] + y
    x1_ref[...] = x1
    h = x1 * _rms_scale(x1) * ng_ref[...]
    h_hi = h.astype(jnp.bfloat16)
    h_ref[...] = h_hi
    h_lo = (h - h_hi.astype(jnp.float32)).astype(jnp.bfloat16)
    wr = wr_ref[...]
    wr_hi = wr.astype(jnp.bfloat16)
    wr_lo = (wr - wr_hi.astype(jnp.float32)).astype(jnp.bfloat16)
    dot = lambda p, q: jnp.dot(p, q, preferred_element_type=jnp.float32)
    logits = dot(h_hi, wr_hi) + (dot(h_lo, wr_hi) + dot(h_hi, wr_lo))
    e = jnp.exp(logits - jnp.max(logits, axis=-1, keepdims=True))
    aff_ref[...] = e / jnp.sum(e, axis=-1, keepdims=True)


def out_proj(x, attn, hy, attn_g, hy_g, w, ffn_g, w_router):
    B, L, D = x.shape
    N = B * L
    tm = min(ROW_TILE, L)
    assert L % tm == 0
    nt = L // tm
    E = w_router.shape[-1]
    slabs = lambda width: pl.BlockSpec((1, width // LANES, tm, LANES), lambda i: (i // nt, 0, i % nt, 0))
    row = lambda width: pl.BlockSpec((tm, width), lambda i: (i, 0))
    full = lambda r, c: pl.BlockSpec((r, c), lambda i: (0, 0))
    return pl.pallas_call(
        _out_proj_kernel,
        grid=(N // tm,),
        in_specs=[row(D), slabs(ATTN_WIDTH), slabs(HYENA_WIDTH), full(1, ATTN_WIDTH), full(1, HYENA_WIDTH),
                  full(D, D), full(1, D), full(D, E)],
        out_specs=[row(D), row(D), row(E)],
        out_shape=[jax.ShapeDtypeStruct((N, D), jnp.float32), jax.ShapeDtypeStruct((N, D), jnp.bfloat16),
                   jax.ShapeDtypeStruct((N, E), jnp.float32)],
        compiler_params=_compiler_params(("parallel",)),
        name="out_proj",
    )(x.reshape(N, D), attn, hy, attn_g, hy_g, w, ffn_g, w_router)


def _ffn_kernel(x_ref, gate_ref, wg_ref, wu_ref, wd_ref, o_ref, acc_ref):
    f = pl.program_id(2)
    x = x_ref[0]
    a = jnp.dot(x, wg_ref[0], preferred_element_type=jnp.float32)
    b = jnp.dot(x, wu_ref[0], preferred_element_type=jnp.float32)
    h = (a * jax.nn.sigmoid(a) * b).astype(jnp.bfloat16)
    y = jnp.dot(h, wd_ref[0], preferred_element_type=jnp.float32)

    @pl.when(f == 0)
    def _():
        acc_ref[...] = y

    @pl.when(f > 0)
    def _():
        acc_ref[...] += y

    @pl.when(f == pl.num_programs(2) - 1)
    def _():
        o_ref[0] = (acc_ref[...] * gate_ref[0]).astype(o_ref.dtype)


def expert_ffn(xe, gates, wg, wu, wd):
    E, C, D = xe.shape
    F = wg.shape[-1]
    tm = min(FFN_TOKEN_TILE, C)
    tf = min(FFN_HIDDEN_TILE, F)
    assert C % tm == 0 and F % tf == 0
    return pl.pallas_call(
        _ffn_kernel,
        grid=(E, C // tm, F // tf),
        in_specs=[
            pl.BlockSpec((1, tm, D), lambda e, i, f: (e, i, 0)),
            pl.BlockSpec((1, tm, 1), lambda e, i, f: (e, i, 0)),
            pl.BlockSpec((1, D, tf), lambda e, i, f: (e, 0, f)),
            pl.BlockSpec((1, D, tf), lambda e, i, f: (e, 0, f)),
            pl.BlockSpec((1, tf, D), lambda e, i, f: (e, f, 0)),
        ],
        out_specs=pl.BlockSpec((1, tm, D), lambda e, i, f: (e, i, 0)),
        out_shape=jax.ShapeDtypeStruct((E, C, D), jnp.bfloat16),
        scratch_shapes=[pltpu.VMEM((tm, D), jnp.float32)],
        compiler_params=_compiler_params(("parallel", "parallel", "arbitrary")),
        name="moe_ffn",
    )(xe, gates, wg, wu, wd)


ROUTE_CHUNK = LANES
ROUTE_WALKS = 8
COMBINE_TOKENS = 512
COMBINE_WINDOW = 128


def _route_select_kernel(aff_ref, tri_ref, ones_ref, low_ref, slot_ref, rank_ref, *, cap):
    a = aff_ref[0]

    def as_float(bits):
        return pltpu.bitcast(jnp.full((8, LANES), bits, jnp.int32), jnp.float32)[0:1, :]

    def bisect(_, lohi):
        lo, hi = lohi
        mid = lo + (hi - lo + 1) // 2
        ok = jnp.sum((a >= as_float(mid)).astype(jnp.int32)) >= cap
        return jnp.where(ok, mid, lo), jnp.where(ok, hi, mid - 1)

    thr_bits, _ = lax.fori_loop(0, 31, bisect, (jnp.int32(0), jnp.int32(0x7F800000)))
    thr = as_float(thr_bits)

    def exclusive_prefix(m):
        x = jnp.where(m, 1.0, 0.0).astype(jnp.bfloat16)
        incl = jnp.dot(x, tri_ref[...], preferred_element_type=jnp.float32)
        tot = jnp.dot(x, ones_ref[...], preferred_element_type=jnp.float32).astype(jnp.bfloat16)
        offs = jnp.dot(low_ref[...], tot, preferred_element_type=jnp.float32)
        return offs + incl - x.astype(jnp.float32)

    gt = a > thr
    eq = a == thr
    need = (cap - jnp.sum(gt.astype(jnp.int32))).astype(jnp.float32)
    sel = gt | (eq & (exclusive_prefix(eq) < need))
    rank = exclusive_prefix(sel).astype(jnp.int32)
    rank_ref[0] = rank
    slot_ref[0] = jnp.where(sel, rank, -1)


def route_select(aff3, cap):
    E, nc, w = aff3.shape
    tri = jnp.asarray(np.triu(np.ones((w, w), np.float32)), jnp.bfloat16)
    ones = jnp.ones((w, w), jnp.bfloat16)
    low = jnp.asarray(np.tril(np.ones((nc, nc), np.float32), -1), jnp.bfloat16)
    blk = pl.BlockSpec((1, nc, w), lambda e: (e, 0, 0))
    full = lambda a: pl.BlockSpec(a.shape, lambda e: (0,) * a.ndim)
    out = jax.ShapeDtypeStruct((E, nc, w), jnp.int32)
    return pl.pallas_call(
        partial(_route_select_kernel, cap=cap),
        grid=(E,),
        in_specs=[blk, full(tri), full(ones), full(low)],
        out_specs=[blk, blk],
        out_shape=[out, out],
        compiler_params=_compiler_params(("parallel",)),
        name="route_select",
    )(aff3, tri, ones, low)


def _route_compact_kernel(offs_ref, slot_ref, aff_ref, idx_ref, gate_ref, acc_ref, *, cap):
    e = pl.program_id(0)
    w = ROUTE_CHUNK
    nc = slot_ref.shape[1]
    span = nc // ROUTE_WALKS
    last_block = (cap - 1) // w
    sub = lax.broadcasted_iota(jnp.int32, (2 * w, w), 0)
    lane = lax.broadcasted_iota(jnp.int32, (1, w), 1)
    zero_row = jnp.zeros((1, w), jnp.float32)

    def block_of(c):
        return jnp.where(c < nc, jnp.minimum(offs_ref[e, jnp.minimum(c, nc - 1)] // w, last_block), last_block)

    def store(walk, block, rows):
        acc_ref[walk, 0, pl.ds(block, 1), :] = rows[0:1] * 256.0 + rows[1:2]
        acc_ref[walk, 1, pl.ds(block, 1), :] = rows[2:3] + rows[3:4] + rows[4:5]

    def chunk(walk, c, carry):
        b0 = block_of(c)
        rel = slot_ref[0, pl.ds(c, 1), :] - b0 * w
        onehot = jnp.where(sub == rel, 1.0, 0.0).astype(jnp.bfloat16)
        tok = c * w + lane
        g = aff_ref[0, pl.ds(c, 1), :]
        g_hi = g.astype(jnp.bfloat16).astype(jnp.float32)
        g_mid = (g - g_hi).astype(jnp.bfloat16).astype(jnp.float32)
        g_lo = g - g_hi - g_mid
        vals = jnp.concatenate([(tok >> 8).astype(jnp.float32), (tok & 255).astype(jnp.float32), g_hi, g_mid,
                                g_lo, zero_row, zero_row, zero_row], axis=0).astype(jnp.bfloat16)
        res = lax.dot_general(vals, onehot, (((1,), (1,)), ((), ())), preferred_element_type=jnp.float32)
        total = carry + res[:, :w]
        store(walk, b0, total)
        return jnp.where(block_of(c + 1) == b0, total, res[:, w:])

    acc_ref[...] = jnp.zeros_like(acc_ref)

    def step(j, carries):
        return tuple(chunk(q, q * span + j, carries[q]) for q in range(ROUTE_WALKS))

    zeros = jnp.zeros((8, w), jnp.float32)
    carries = lax.fori_loop(0, span, step, (zeros,) * ROUTE_WALKS, unroll=2)
    for q in range(ROUTE_WALKS):
        store(q, block_of((q + 1) * span), carries[q])
    idx_ref[0] = sum(acc_ref[q, 0] for q in range(ROUTE_WALKS)).astype(jnp.int32)
    gate_ref[0] = sum(acc_ref[q, 1] for q in range(ROUTE_WALKS))


def route_compact(slot3, aff3, offs, cap):
    E, nc, w = slot3.shape
    assert cap % w == 0 and nc % ROUTE_WALKS == 0
    blk = pl.BlockSpec((1, nc, w), lambda e, offs: (e, 0, 0))
    out_blk = pl.BlockSpec((1, cap // w, w), lambda e, offs: (e, 0, 0))
    idx, gates = pl.pallas_call(
        partial(_route_compact_kernel, cap=cap),
        grid_spec=pltpu.PrefetchScalarGridSpec(
            num_scalar_prefetch=1, grid=(E,), in_specs=[blk, blk], out_specs=[out_blk, out_blk],
            scratch_shapes=[pltpu.VMEM((ROUTE_WALKS, 2, cap // w, w), jnp.float32)]),
        out_shape=[jax.ShapeDtypeStruct((E, cap // w, w), jnp.int32),
                   jax.ShapeDtypeStruct((E, cap // w, w), jnp.float32)],
        compiler_params=_compiler_params(("arbitrary",)),
        name="route_compact",
    )(offs, slot3, aff3)
    return idx.reshape(E, cap), gates.reshape(E, cap)


def _combine_kernel(starts_ref, x_ref, slot_ref, g_ref, ye_hbm, o_ref, stage_ref, y_ref, sems, *, cap):
    j = pl.program_id(0)
    n_tiles = pl.num_programs(0)
    n_experts = slot_ref.shape[1]
    t = x_ref.shape[0]
    win = COMBINE_WINDOW
    buf = j % 2

    def window_copy(b, e, start):
        return pltpu.make_async_copy(ye_hbm.at[e, pl.ds(start, win), :], stage_ref.at[b, pl.ds(e * win, win), :],
                                     sems.at[b, e])

    def aligned(start):
        return pl.multiple_of(jnp.minimum((start // 16) * 16, cap - win), 16)

    def first_windows(tile):
        return [aligned(starts_ref[e, tile]) for e in range(n_experts)]

    @pl.when(j == 0)
    def _():
        for e, start in enumerate(first_windows(0)):
            window_copy(0, e, start).start()

    @pl.when(j + 1 < n_tiles)
    def _():
        for e, start in enumerate(first_windows(j + 1)):
            window_copy(1 - buf, e, start).start()

    bases = first_windows(j)
    lane = lax.broadcasted_iota(jnp.int32, (t, win), 1)
    slot = slot_ref[...]
    onehot = jnp.concatenate(
        [jnp.where(slot[:, e:e + 1] - bases[e] == lane, 1.0, 0.0).astype(jnp.bfloat16) for e in range(n_experts)],
        axis=1)
    for e in range(n_experts):
        window_copy(buf, e, bases[e]).wait()
    y_ref[...] = jnp.dot(onehot, stage_ref[buf], preferred_element_type=jnp.float32)

    for e in range(n_experts):
        end = starts_ref[e, j + 1]
        n_more = jnp.maximum(end - bases[e] + win - 1, 0) // win - 1

        def more(k, carry, e=e):
            first = bases[e] + (k + 1) * win
            start = aligned(first)
            cp = window_copy(buf, e, start)
            cp.start()
            cp.wait()
            col = slot_ref[:, e:e + 1]
            hit = (col - start == lane) & (col >= first)
            rows = stage_ref[buf, pl.ds(e * win, win), :]
            y_ref[...] += jnp.dot(jnp.where(hit, 1.0, 0.0).astype(jnp.bfloat16), rows,
                                  preferred_element_type=jnp.float32)
            return carry

        lax.fori_loop(0, n_more, more, 0)

    x = x_ref[...] + y_ref[...]
    o_ref[...] = x * _rms_scale(x) * g_ref[...]


def combine(x1, slot_t, starts, ye, g):
    N, D = x1.shape
    E, cap, _ = ye.shape
    t = min(COMBINE_TOKENS, N)
    assert N % t == 0 and cap % COMBINE_WINDOW == 0
    return pl.pallas_call(
        partial(_combine_kernel, cap=cap),
        grid_spec=pltpu.PrefetchScalarGridSpec(
            num_scalar_prefetch=1, grid=(N // t,),
            in_specs=[pl.BlockSpec((t, D), lambda j, s: (j, 0)), pl.BlockSpec((t, E), lambda j, s: (j, 0)),
                      pl.BlockSpec((1, D), lambda j, s: (0, 0)), pl.BlockSpec(memory_space=pl.ANY)],
            out_specs=pl.BlockSpec((t, D), lambda j, s: (j, 0)),
            scratch_shapes=[pltpu.VMEM((2, E * COMBINE_WINDOW, D), jnp.bfloat16), pltpu.VMEM((t, D), jnp.float32),
                            pltpu.SemaphoreType.DMA((2, E))]),
        out_shape=jax.ShapeDtypeStruct((N, D), jnp.float32),
        compiler_params=_compiler_params(("arbitrary",)),
        name="moe_combine",
    )(starts, x1, slot_t, g, ye)


DFT_P = 128
SLAB_PAD = 8
X_PITCH = DFT_P + SLAB_PAD
A_PITCH = 2 * DFT_P + SLAB_PAD
HY_SLABS = HYENA_WIDTH // LANES
U_SLABS = 3 * HY_SLABS
SLOW_STAGE_UNROLL = 64
SPECTRAL_UNROLL = 11


def _dft_sizes(M):
    q = M // DFT_P
    ka = q // 2 + 1
    kp = -(-ka // 8) * 8
    return q, ka, kp


def _dft_constants(M):
    p = DFT_P
    q, ka, kp = _dft_sizes(M)
    k = np.arange(ka)

    def slow_fwd(n_ts):
        ang = 2.0 * np.pi * ((k[:, None] * np.arange(n_ts)[None, :]) % q) / q
        m = np.zeros((2 * kp, n_ts))
        m[:ka] = np.cos(ang)
        m[kp:kp + ka] = -np.sin(ang)
        return m

    ts = np.arange(q // 2)
    ang = 2.0 * np.pi * ((ts[:, None] * k[None, :]) % q) / q
    c = np.where((k == 0) | (k == q // 2), 1.0, 2.0)[None, :]
    slow_inv = np.zeros((q // 2, 2 * kp))
    slow_inv[:, :ka] = c * np.cos(ang) / M
    slow_inv[:, kp:kp + ka] = -c * np.sin(ang) / M

    kb = np.arange(p)
    tf = np.arange(p)
    idx = (tf[None, None, :] * (k[:, None, None] + q * kb[None, :, None])) % M
    wr = np.cos(2.0 * np.pi * idx / M)
    wi = -np.sin(2.0 * np.pi * idx / M)
    w2 = np.concatenate([np.concatenate([wr, -wi], axis=2), np.concatenate([wi, wr], axis=2)], axis=1)
    bf = lambda a: jnp.asarray(a, jnp.float32).astype(jnp.bfloat16)
    return {"fa_half": bf(slow_fwd(q // 2)), "fa_full": bf(slow_fwd(q)), "fi": bf(slow_inv),
            "w2": bf(w2), "v2": bf(np.transpose(w2, (0, 2, 1)))}


def _slow_stage_forward(xs_ref, ag_ref, fa_ref, n_ts, kp):
    fa = fa_ref[...]

    def body(tf, carry):
        xt = xs_ref[pl.ds(tf, n_ts, stride=X_PITCH), :].astype(jnp.bfloat16)
        a = jnp.dot(fa, xt, preferred_element_type=jnp.float32)
        ag_ref[pl.ds(tf, kp, stride=A_PITCH), :] = a[:kp]
        ag_ref[pl.ds(DFT_P + tf, kp, stride=A_PITCH), :] = a[kp:]
        return carry

    lax.fori_loop(0, DFT_P, body, 0, unroll=SLOW_STAGE_UNROLL)


def _slow_stage_inverse(ag_ref, ys_ref, fi_ref, n_ts, kp):
    fi = fi_ref[...]

    def body(tf, carry):
        gr = ag_ref[pl.ds(tf, kp, stride=A_PITCH), :]
        gi = ag_ref[pl.ds(DFT_P + tf, kp, stride=A_PITCH), :]
        g = jnp.concatenate([gr, gi], axis=0).astype(jnp.bfloat16)
        ys_ref[pl.ds(tf, n_ts, stride=X_PITCH), :] = jnp.dot(fi, g, preferred_element_type=jnp.float32)
        return carry

    lax.fori_loop(0, DFT_P, body, 0, unroll=SLOW_STAGE_UNROLL)


def _fast_stage(ag_ref, w2_ref, k):
    r0 = pl.multiple_of(k * A_PITCH, 8)
    a2 = ag_ref[pl.ds(r0, 2 * DFT_P), :].astype(jnp.bfloat16)
    return r0, jnp.dot(w2_ref[k], a2, preferred_element_type=jnp.float32)


def _spectral_product(ag_ref, w2_ref, v2_ref, kspec_ref, ka):
    p = DFT_P

    def block(k):
        r0, x2 = _fast_stage(ag_ref, w2_ref, k)
        kk = kspec_ref[0, 0, pl.ds(pl.multiple_of(k * 2 * p, 2 * p), 2 * p), :]
        xr, xi, kr, ki = x2[:p], x2[p:], kk[:p], kk[p:]
        y2 = jnp.concatenate([xr * kr - xi * ki, xr * ki + xi * kr], axis=0).astype(jnp.bfloat16)
        return r0, jnp.dot(v2_ref[k], y2, preferred_element_type=jnp.float32)

    def body(j, carry):
        done = [block(SPECTRAL_UNROLL * j + i) for i in range(SPECTRAL_UNROLL)]
        for r0, g2 in done:
            ag_ref[pl.ds(r0, 2 * p), :] = g2
        return carry

    lax.fori_loop(0, ka // SPECTRAL_UNROLL, body, 0)
    rest = [block(k) for k in range(ka - ka % SPECTRAL_UNROLL, ka)]
    for r0, g2 in rest:
        ag_ref[pl.ds(r0, 2 * p), :] = g2


def _hyena_kernel(uv_ref, u1_ref, u2_ref, sw_ref, sb_ref, skip_ref, k0_ref, k1_ref, fa_ref, fi_ref, w2_ref,
                  v2_ref, o_ref, xs_ref, ys_ref, ag_ref, *, L):
    p = DFT_P
    c = pl.program_id(0)
    n_ts = L // p
    _, ka, kp = _dft_sizes(2 * L)
    row = lax.broadcasted_iota(jnp.int32, (p, LANES), 0)

    def short_conv_block(u_ref, group, ts):
        r0 = pl.multiple_of(ts * p, p)
        mid = u_ref[0, 0, pl.ds(r0, p), :]
        prev8 = u_ref[0, 0, pl.ds(pl.multiple_of(jnp.maximum(r0 - 8, 0), 8), 8), :]
        next8 = u_ref[0, 0, pl.ds(pl.multiple_of(jnp.minimum(r0 + p, L - 8), 8), 8), :]
        has_prev = (r0 > 0).astype(jnp.float32)
        has_next = (r0 + p < L).astype(jnp.float32)
        up = jnp.where(row == 0, prev8[7:8, :] * has_prev, pltpu.roll(mid, 1, 0))
        dn = jnp.where(row == p - 1, next8[0:1, :] * has_next, pltpu.roll(mid, p - 1, 0))
        ch = group * HY_SLABS + c
        w = lambda tap: sw_ref[pl.ds(tap * U_SLABS + ch, 1), :]
        return up * w(0) + mid * w(1) + dn * w(2) + sb_ref[pl.ds(ch, 1), :]

    def long_conv(kspec_ref):
        _slow_stage_forward(xs_ref, ag_ref, fa_ref, n_ts, kp)
        _spectral_product(ag_ref, w2_ref, v2_ref, kspec_ref, ka)
        _slow_stage_inverse(ag_ref, ys_ref, fi_ref, n_ts, kp)

    def blocks(fn):
        def body(ts, carry):
            fn(ts, pl.ds(pl.multiple_of(ts * X_PITCH, 8), p))
            return carry
        lax.fori_loop(0, n_ts, body, 0)

    def stage_v(ts, rows):
        xs_ref[rows, :] = short_conv_block(uv_ref, 0, ts)

    def stage_z(ts, rows):
        v = xs_ref[rows, :]
        y1 = ys_ref[rows, :] + skip_ref[pl.ds(c, 1), :] * v
        xs_ref[rows, :] = short_conv_block(u1_ref, 1, ts) * y1

    def stage_out(ts, rows):
        z = xs_ref[rows, :]
        y2 = ys_ref[rows, :] + skip_ref[pl.ds(HY_SLABS + c, 1), :] * z
        o_ref[0, 0, pl.ds(pl.multiple_of(ts * p, p), p), :] = short_conv_block(u2_ref, 2, ts) * y2

    blocks(stage_v)
    long_conv(k0_ref)
    blocks(stage_z)
    long_conv(k1_ref)
    blocks(stage_out)


def hyena_mix(u, sw, sb, skip, kspec, consts):
    B, n_slabs, L, _ = u.shape
    q, ka, kp = _dft_sizes(2 * L)
    n_ts = L // DFT_P
    u_spec = lambda g: pl.BlockSpec((1, 1, L, LANES), lambda c, b: (b, g * HY_SLABS + c, 0, 0))
    k_spec = lambda o: pl.BlockSpec((1, 1, ka * 2 * DFT_P, LANES), lambda c, b: (o, c, 0, 0),
                                    pipeline_mode=pl.Buffered(1))
    const = lambda a: pl.BlockSpec(a.shape, lambda c, b: (0,) * a.ndim, pipeline_mode=pl.Buffered(1))
    small = lambda a: pl.BlockSpec(a.shape, lambda c, b: (0,) * a.ndim)
    fa, fi, w2, v2 = consts["fa_half"], consts["fi"], consts["w2"], consts["v2"]
    return pl.pallas_call(
        partial(_hyena_kernel, L=L),
        grid=(HY_SLABS, B),
        in_specs=[u_spec(0), u_spec(1), u_spec(2), small(sw), small(sb), small(skip), k_spec(0), k_spec(1),
                  small(fa), small(fi), const(w2), const(v2)],
        out_specs=pl.BlockSpec((1, 1, L, LANES), lambda c, b: (b, c, 0, 0)),
        out_shape=jax.ShapeDtypeStruct((B, HY_SLABS, L, LANES), jnp.float32),
        scratch_shapes=[pltpu.VMEM((n_ts * X_PITCH, LANES), jnp.float32),
                        pltpu.VMEM((n_ts * X_PITCH, LANES), jnp.float32),
                        pltpu.VMEM((kp * A_PITCH, LANES), jnp.float32)],
        compiler_params=_compiler_params(("arbitrary", "arbitrary")),
        name="hyena_mix",
    )(u, u, u, sw, sb, skip, kspec, kspec, fa, fi, w2, v2)


def _filter_spectrum_kernel(kern_ref, fa_ref, w2_ref, o_ref, xs_ref, ag_ref, *, M):
    p = DFT_P
    q, ka, kp = _dft_sizes(M)

    def load(ts, carry):
        xs_ref[pl.ds(pl.multiple_of(ts * X_PITCH, 8), p), :] = kern_ref[0, 0, pl.ds(pl.multiple_of(ts * p, p), p), :]
        return carry

    lax.fori_loop(0, q, load, 0)
    _slow_stage_forward(xs_ref, ag_ref, fa_ref, q, kp)

    def body(k, carry):
        _, x2 = _fast_stage(ag_ref, w2_ref, k)
        o_ref[0, 0, pl.ds(pl.multiple_of(k * 2 * p, 2 * p), 2 * p), :] = x2
        return carry

    lax.fori_loop(0, ka, body, 0)


def filter_spectrum(kern, consts):
    n_o, n_c, M, _ = kern.shape
    q, ka, kp = _dft_sizes(M)
    fa, w2 = consts["fa_full"], consts["w2"]
    full = lambda a: pl.BlockSpec(a.shape, lambda o, c: (0,) * a.ndim)
    return pl.pallas_call(
        partial(_filter_spectrum_kernel, M=M),
        grid=(n_o, n_c),
        in_specs=[pl.BlockSpec((1, 1, M, LANES), lambda o, c: (o, c, 0, 0)), full(fa), full(w2)],
        out_specs=pl.BlockSpec((1, 1, ka * 2 * DFT_P, LANES), lambda o, c: (o, c, 0, 0)),
        out_shape=jax.ShapeDtypeStruct((n_o, n_c, ka * 2 * DFT_P, LANES), jnp.float32),
        scratch_shapes=[pltpu.VMEM((q * X_PITCH, LANES), jnp.float32),
                        pltpu.VMEM((kp * A_PITCH, LANES), jnp.float32)],
        compiler_params=_compiler_params(("arbitrary", "arbitrary")),
        name="filter_spectrum",
    )(kern, fa, w2)


FILTER_ROW_TILE = 512


def _filter_kernel(bands_ref, w1t_ref, w1c_ref, w1s_ref, b1_ref, w2_ref, b2_ref, w3_ref, b3_ref, w4_ref, freq_ref,
                   absd_ref, o_ref, *, L):
    tr = o_ref.shape[2]
    hp = lax.Precision.HIGHEST
    rho = pl.program_id(0) * tr + lax.broadcasted_iota(jnp.int32, (tr, 1), 0)
    tau = jnp.where(rho < L, rho, 2 * L - rho).astype(jnp.float32)
    t = tau * (1.0 / (L - 1))
    ang = (2.0 * math.pi * tau / L) * bands_ref[...]
    dot = lambda a, b: jnp.dot(a, b, preferred_element_type=jnp.float32, precision=hp)
    fr = freq_ref[...]
    pre = t * w1t_ref[...] + dot(jnp.cos(ang), w1c_ref[...]) - dot(jnp.sin(ang), w1s_ref[...]) + b1_ref[...]
    h = jnp.sin(fr * pre)
    h = jnp.sin(fr * (dot(h, w2_ref[...]) + b2_ref[...]))
    h = jnp.sin(fr * (dot(h, w3_ref[...]) + b3_ref[...]))
    hh = dot(h, w4_ref[...])
    decay = jnp.exp(-t * absd_ref[...])
    fwd = rho < L
    nonzero = rho != L
    w = HYENA_WIDTH
    for o in range(HYENA_ORDER):
        for c in range(HY_SLABS):
            f = hh[:, (2 * o) * w + c * LANES:(2 * o) * w + (c + 1) * LANES]
            b = hh[:, (2 * o + 1) * w + c * LANES:(2 * o + 1) * w + (c + 1) * LANES]
            val = jnp.where(fwd, f, b) * decay[:, c * LANES:(c + 1) * LANES]
            o_ref[o, c] = jnp.where(nonzero, val, 0.0)


def hyena_kernels(L, w1, b1, w2, b2, w3, b3, w4, freq):
    M = 2 * L
    tr = min(FILTER_ROW_TILE, M)
    bands = np.zeros((1, LANES), np.float32)
    bands[0, :FILTER_BANDS] = np.linspace(1e-4, FILTER_BANDS - 1, FILTER_BANDS, dtype=np.float32)
    pad_rows = lambda a: jnp.zeros((LANES, a.shape[1]), jnp.float32).at[:a.shape[0]].set(a)
    min_decay = math.log(DECAY_TARGET) / FAST_DECAY_PCT
    max_decay = math.log(DECAY_TARGET) / SLOW_DECAY_PCT
    absd = np.abs(np.linspace(min_decay, max_decay, HYENA_WIDTH, dtype=np.float32))[None, :]
    args = [jnp.asarray(bands), w1[0:1], pad_rows(w1[1:1 + FILTER_BANDS]), pad_rows(w1[1 + FILTER_BANDS:]),
            b1[None, :], w2, b2[None, :], w3, b3[None, :], w4, freq[None, :], jnp.asarray(absd)]
    full = lambda a: pl.BlockSpec(a.shape, lambda i: (0,) * a.ndim)
    return pl.pallas_call(
        partial(_filter_kernel, L=L),
        grid=(M // tr,),
        in_specs=[full(a) for a in args],
        out_specs=pl.BlockSpec((HYENA_ORDER, HY_SLABS, tr, LANES), lambda i: (0, 0, i, 0)),
        out_shape=jax.ShapeDtypeStruct((HYENA_ORDER, HY_SLABS, M, LANES), jnp.float32),
        compiler_params=_compiler_params(("parallel",)),
        name="hyena_filter",
    )(*args)


def hyena_operator(u, p):
    L = u.shape[2]
    consts = _dft_constants(2 * L)
    kern = hyena_kernels(L, p["filt_w1"], p["filt_b1"], p["filt_w2"], p["filt_b2"], p["filt_w3"], p["filt_b3"],
                         p["filt_w4"], p["filt_freq"])
    kspec = filter_spectrum(kern, consts)
    sw = p["short_w"].reshape(SHORT_CONV * U_SLABS, LANES)
    sb = p["short_b"].reshape(U_SLABS, LANES)
    skip = p["hyena_skip"].reshape(HYENA_ORDER * HY_SLABS, LANES)
    return hyena_mix(u, sw, sb, skip, kspec, consts)


def expert_choice_ffn(x1, h, affinity, p):
    N, D = h.shape
    E = affinity.shape[1]
    cap = EC_CAPACITY * N // N_EXPERTS
    aff3 = affinity.T.reshape(E, N // ROUTE_CHUNK, ROUTE_CHUNK)
    slot3, rank3 = route_select(aff3, cap)
    idx, gates = route_compact(slot3, aff3, rank3[:, :, 0], cap)
    ye = expert_ffn(h[idx], gates[..., None], p["w_gate"], p["w_up"], p["w_down"])
    t = min(COMBINE_TOKENS, N)
    starts = jnp.concatenate([rank3.reshape(E, N)[:, ::t], jnp.full((E, 1), cap, jnp.int32)], axis=1)
    return combine(x1, slot3.reshape(E, N).T, starts, ye, p["final_g"])


def encoder_trunk(x, p):
    B, L, D = x.shape
    qkv, u = in_proj(x, p["norm_mix_g"], p["w_in"])
    attn = dilated_attention(qkv)
    hy = hyena_operator(u, p)
    x1, h, affinity = out_proj(x, attn, hy, p["attn_out_g"], p["hyena_out_g"], p["w_out"], p["norm_ffn_g"],
                               p["w_router"])
    return expert_choice_ffn(x1, h, affinity, p).reshape(B, L, D)


def kernel(x_prompt, x_sample, norm_mix_g, w_in, short_w, short_b, filt_w1, filt_b1, filt_w2, filt_b2, filt_w3, filt_b3, filt_w4, filt_freq, hyena_skip, attn_out_g, hyena_out_g, w_out, norm_ffn_g, w_router, w_gate, w_up, w_down, final_g):
    bf16 = jnp.bfloat16
    p = {
        "norm_mix_g": norm_mix_g, "w_in": w_in[0].astype(bf16),
        "short_w": short_w[0], "short_b": short_b[0],
        "filt_w1": filt_w1[0], "filt_b1": filt_b1[0], "filt_w2": filt_w2[0], "filt_b2": filt_b2[0],
        "filt_w3": filt_w3[0], "filt_b3": filt_b3[0], "filt_w4": filt_w4[0], "filt_freq": filt_freq[0],
        "hyena_skip": hyena_skip[0], "attn_out_g": attn_out_g, "hyena_out_g": hyena_out_g,
        "w_out": w_out[0].astype(bf16), "norm_ffn_g": norm_ffn_g, "w_router": w_router[0],
        "w_gate": w_gate[0].astype(bf16), "w_up": w_up[0].astype(bf16), "w_down": w_down[0].astype(bf16),
        "final_g": final_g.reshape(1, D_MODEL),
    }
    return (encoder_trunk(x_prompt, p), encoder_trunk(x_sample, p))
```

```python
import math
from functools import partial

import jax
import jax.numpy as jnp
import numpy as np
from jax import lax
from jax.experimental import pallas as pl
from jax.experimental.pallas import tpu as pltpu

D_MODEL = 1024
HEAD_DIM = 64
N_ATTN_HEADS = 8
ATTN_WIDTH = N_ATTN_HEADS * HEAD_DIM
HYENA_WIDTH = D_MODEL - ATTN_WIDTH
HYENA_ORDER = 2
SHORT_CONV = 3
FILTER_EMB = 33
FILTER_BANDS = (FILTER_EMB - 1) // 2
FAST_DECAY_PCT = 0.3
SLOW_DECAY_PCT = 1.5
DECAY_TARGET = 1e-2
DILATED_PATTERNS = ((128, 1), (512, 4), (2048, 16))
N_EXPERTS = 16
EC_CAPACITY = 2
D_EXPERT = 2816
NORM_EPS = 1e-6
MASK_VALUE = -1e30

LANES = 128
VMEM_LIMIT_BYTES = 56 * 1024 * 1024

ROW_TILE = 512
FFN_TOKEN_TILE = 1024
FFN_HIDDEN_TILE = D_EXPERT // 2
ATTN_Q_TILE = 128
ATTN_TILES_PER_TRIP = 16


def _compiler_params(semantics):
    return pltpu.CompilerParams(dimension_semantics=semantics, vmem_limit_bytes=VMEM_LIMIT_BYTES)


def _rms_scale(x):
    return lax.rsqrt(jnp.mean(x * x, axis=-1, keepdims=True) + NORM_EPS)


def _in_proj_kernel(x_ref, g_ref, w_ref, qkv_ref, u_ref):
    x = x_ref[...]
    h = (x * _rms_scale(x) * g_ref[...]).astype(jnp.bfloat16)
    a = ATTN_WIDTH
    qkv = jnp.dot(h, w_ref[:, :3 * a], preferred_element_type=jnp.float32)
    for s in range(qkv_ref.shape[1]):
        cols = qkv[:, s * LANES:(s + 1) * LANES]
        qkv_ref[0, s] = cols * (HEAD_DIM ** -0.5) if s < a // LANES else cols
    u = jnp.dot(h, w_ref[:, 3 * a:], preferred_element_type=jnp.float32)
    for s in range(u_ref.shape[1]):
        u_ref[0, s] = u[:, s * LANES:(s + 1) * LANES]


def in_proj(x, g, w):
    B, L, D = x.shape
    N = B * L
    tm = min(ROW_TILE, L)
    assert L % tm == 0
    nt = L // tm
    a3, h3 = 3 * ATTN_WIDTH, 3 * HYENA_WIDTH
    slabs = lambda width: pl.BlockSpec((1, width // LANES, tm, LANES), lambda i: (i // nt, 0, i % nt, 0))
    return pl.pallas_call(
        _in_proj_kernel,
        grid=(N // tm,),
        in_specs=[
            pl.BlockSpec((tm, D), lambda i: (i, 0)),
            pl.BlockSpec((1, D), lambda i: (0, 0)),
            pl.BlockSpec((D, a3 + h3), lambda i: (0, 0)),
        ],
        out_specs=[slabs(a3), slabs(h3)],
        out_shape=[jax.ShapeDtypeStruct((B, a3 // LANES, L, LANES), jnp.float32),
                   jax.ShapeDtypeStruct((B, h3 // LANES, L, LANES), jnp.float32)],
        compiler_params=_compiler_params(("parallel",)),
        name="in_proj",
    )(x.reshape(N, D), g, w)


def _alibi_slope(h):
    return 2.0 ** (-8.0 * (h + 1) / N_ATTN_HEADS)


def _attn_tiling(L, window, dilation):
    radius = window // (2 * dilation)
    n = L // dilation
    tq = min(ATTN_Q_TILE, n)
    tk = min(tq + 2 * radius, n)
    return radius, n, tq, tk


def _attn_kernel(q_ref, k_ref, v_ref, o_ref, ob_ref, lse_ref, *bias_refs, L, hp_axis):
    hp = pl.program_id(hp_axis)

    def pick(values):
        out = jnp.float32(values[-1])
        for j in range(len(values) - 2, -1, -1):
            out = jnp.where(hp == j, jnp.float32(values[j]), out)
        return out

    n_pairs = ATTN_WIDTH // LANES
    slope_a = pick([_alibi_slope(2 * j) for j in range(n_pairs)])
    slope_b = pick([_alibi_slope(2 * j + 1) for j in range(n_pairs)])

    for branch, (window, dilation) in enumerate(DILATED_PATTERNS):
        radius, n, tq, tk = _attn_tiling(L, window, dilation)
        tiles = n // tq

        lane = lax.broadcasted_iota(jnp.int32, (tq, LANES), 1)
        low = lane < HEAD_DIM
        row2 = lax.broadcasted_iota(jnp.int32, (2 * tq, tk), 0)
        col2 = lax.broadcasted_iota(jnp.int32, (2 * tq, tk), 1)
        base_rel = col2 - jnp.where(row2 >= tq, row2 - tq, row2)
        second_col = lax.broadcasted_iota(jnp.int32, (2 * tq, 1), 0) >= tq
        slope = jnp.where(second_col, slope_b, slope_a) * float(dilation)

        offsets = {q0 - min(max(q0 - radius, 0), n - tk) for q0 in range(0, n, tq)}
        assert offsets <= {0, radius, 2 * radius}
        bias_ref = bias_refs[branch]
        for case in range(3):
            absrel = jnp.abs(base_rel - case * radius)
            bias_ref[case] = jnp.where(absrel <= radius, -(slope * absrel.astype(jnp.float32)), MASK_VALUE)

        def tile(it, carry, dilation=dilation, radius=radius, n=n, tq=tq, tk=tk, tiles=tiles,
                 low=low, lane=lane, bias_ref=bias_ref):
            r = it // tiles
            q0 = (it % tiles) * tq
            ks = jnp.clip(q0 - radius, 0, n - tk)
            q_rows = pl.ds(r + dilation * q0, tq, stride=dilation)
            k_rows = pl.ds(r + dilation * ks, tk, stride=dilation)
            q = q_ref[0, 0, q_rows, :].astype(jnp.bfloat16)
            kw = k_ref[0, 0, k_rows, :].astype(jnp.bfloat16)
            vw = v_ref[0, 0, k_rows, :].astype(jnp.bfloat16)
            zero = jnp.zeros_like(q)
            q2 = jnp.concatenate([jnp.where(low, q, zero), jnp.where(low, zero, q)], axis=0)
            s = lax.dot_general(q2, kw, (((1,), (1,)), ((), ())), preferred_element_type=jnp.float32)
            s = s + bias_ref[(q0 - ks) // radius]
            m = jnp.max(s, axis=-1, keepdims=True)
            p = jnp.exp(s - m)
            l = jnp.sum(p, axis=-1, keepdims=True)
            o2 = jnp.dot(p.astype(jnp.bfloat16), vw, preferred_element_type=jnp.float32) / l
            lse2 = m + jnp.log(l)
            o = jnp.where(low, o2[:tq], o2[tq:])
            lse = jnp.where(lane == 0, lse2[:tq], jnp.where(lane == 1, lse2[tq:], 0.0))
            return q_rows, o, lse

        per_trip = min(ATTN_TILES_PER_TRIP, dilation * tiles)

        def tiles_body(j, carry, tile=tile, per_trip=per_trip, branch=branch):
            done = [tile(per_trip * j + i, carry) for i in range(per_trip)]
            for q_rows, o, lse in done:
                ob_ref[branch, q_rows, :] = o
                lse_ref[branch, q_rows, :] = lse
            return carry

        assert (dilation * tiles) % per_trip == 0
        lax.fori_loop(0, dilation * tiles // per_trip, tiles_body, 0)

    n_branches = len(DILATED_PATTERNS)
    tm = min(ATTN_Q_TILE, L)
    src = lax.broadcasted_iota(jnp.int32, (LANES, LANES), 0)
    dst = lax.broadcasted_iota(jnp.int32, (LANES, LANES), 1)
    spread = jnp.where(src * HEAD_DIM <= dst, jnp.where(dst < (src + 1) * HEAD_DIM, 1.0, 0.0), 0.0)
    spread = spread.astype(jnp.bfloat16)

    def mix(i, carry):
        rows = pl.ds(pl.multiple_of(i * tm, tm), tm)
        lses = [lse_ref[b, rows, :] for b in range(n_branches)]
        mx = lses[0]
        for x in lses[1:]:
            mx = jnp.maximum(mx, x)
        es = [jnp.exp(x - mx) for x in lses]
        inv = 1.0 / sum(es)
        w = jnp.concatenate([e * inv for e in es], axis=0)
        w_hi = w.astype(jnp.bfloat16)
        w_lo = (w - w_hi.astype(jnp.float32)).astype(jnp.bfloat16)
        wide = (jnp.dot(w_hi, spread, preferred_element_type=jnp.float32)
                + jnp.dot(w_lo, spread, preferred_element_type=jnp.float32))
        o = 0.0
        for b in range(n_branches):
            o = o + ob_ref[b, rows, :] * wide[b * tm:(b + 1) * tm]
        o_ref[0, 0, rows, :] = o
        return carry

    lax.fori_loop(0, L // tm, mix, 0, unroll=4)


def dilated_attention(qkv):
    B, _, L, _ = qkv.shape
    n_pairs = ATTN_WIDTH // LANES
    bias_scratch = []
    for window, dilation in DILATED_PATTERNS:
        radius, n, tq, tk = _attn_tiling(L, window, dilation)
        assert L % dilation == 0 and n % tq == 0
        bias_scratch.append(pltpu.VMEM((3, 2 * tq, tk), jnp.float32))
    branch_state = pltpu.VMEM((len(DILATED_PATTERNS), L, LANES), jnp.float32)
    spec = lambda g: pl.BlockSpec((1, 1, L, LANES), lambda b, hp: (b, g * n_pairs + hp, 0, 0))
    return pl.pallas_call(
        partial(_attn_kernel, L=L, hp_axis=1),
        grid=(B, n_pairs),
        in_specs=[spec(0), spec(1), spec(2)],
        out_specs=pl.BlockSpec((1, 1, L, LANES), lambda b, hp: (b, hp, 0, 0)),
        out_shape=jax.ShapeDtypeStruct((B, n_pairs, L, LANES), jnp.float32),
        scratch_shapes=[branch_state, branch_state] + bias_scratch,
        compiler_params=_compiler_params(("parallel", "arbitrary")),
        name="dilated_attn",
    )(qkv, qkv, qkv)


def _out_proj_kernel(x_ref, attn_ref, hy_ref, ag_ref, hg_ref, w_ref, ng_ref, wr_ref, x1_ref, h_ref, aff_ref):
    a = ATTN_WIDTH

    def group_norm(ref, g_ref):
        v = jnp.concatenate([ref[0, s] for s in range(ref.shape[1])], axis=-1)
        return (v * _rms_scale(v) * g_ref[...]).astype(jnp.bfloat16)

    y = jnp.dot(group_norm(attn_ref, ag_ref), w_ref[:a, :], preferred_element_type=jnp.float32)
    y = y + jnp.dot(group_norm(hy_ref, hg_ref), w_ref[a:, :], preferred_element_type=jnp.float32)
    x1 = x_ref[...] + y
    x1_ref[...] = x1
    h = x1 * _rms_scale(x1) * ng_ref[...]
    h_hi = h.astype(jnp.bfloat16)
    h_ref[...] = h_hi
    h_lo = (h - h_hi.astype(jnp.float32)).astype(jnp.bfloat16)
    wr = wr_ref[...]
    wr_hi = wr.astype(jnp.bfloat16)
    wr_lo = (wr - wr_hi.astype(jnp.float32)).astype(jnp.bfloat16)
    dot = lambda p, q: jnp.dot(p, q, preferred_element_type=jnp.float32)
    logits = dot(h_hi, wr_hi) + (dot(h_lo, wr_hi) + dot(h_hi, wr_lo))
    e = jnp.exp(logits - jnp.max(logits, axis=-1, keepdims=True))
    aff_ref[...] = e / jnp.sum(e, axis=-1, keepdims=True)


def out_proj(x, attn, hy, attn_g, hy_g, w, ffn_g, w_router):
    B, L, D = x.shape
    N = B * L
    tm = min(ROW_TILE, L)
    assert L % tm == 0
    nt = L // tm
    E = w_router.shape[-1]
    slabs = lambda width: pl.BlockSpec((1, width // LANES, tm, LANES), lambda i: (i // nt, 0, i % nt, 0))
    row = lambda width: pl.BlockSpec((tm, width), lambda i: (i, 0))
    full = lambda r, c: pl.BlockSpec((r, c), lambda i: (0, 0))
    return pl.pallas_call(
        _out_proj_kernel,
        grid=(N // tm,),
        in_specs=[row(D), slabs(ATTN_WIDTH), slabs(HYENA_WIDTH), full(1, ATTN_WIDTH), full(1, HYENA_WIDTH),
                  full(D, D), full(1, D), full(D, E)],
        out_specs=[row(D), row(D), row(E)],
        out_shape=[jax.ShapeDtypeStruct((N, D), jnp.float32), jax.ShapeDtypeStruct((N, D), jnp.bfloat16),
                   jax.ShapeDtypeStruct((N, E), jnp.float32)],
        compiler_params=_compiler_params(("parallel",)),
        name="out_proj",
    )(x.reshape(N, D), attn, hy, attn_g, hy_g, w, ffn_g, w_router)


def _ffn_kernel(x_ref, gate_ref, wg_ref, wu_ref, wd_ref, o_ref, acc_ref):
    f = pl.program_id(2)
    x = x_ref[0]
    a = jnp.dot(x, wg_ref[0], preferred_element_type=jnp.float32)
    b = jnp.dot(x, wu_ref[0], preferred_element_type=jnp.float32)
    h = (a * jax.nn.sigmoid(a) * b).astype(jnp.bfloat16)
    y = jnp.dot(h, wd_ref[0], preferred_element_type=jnp.float32)

    @pl.when(f == 0)
    def _():
        acc_ref[...] = y

    @pl.when(f > 0)
    def _():
        acc_ref[...] += y

    @pl.when(f == pl.num_programs(2) - 1)
    def _():
        o_ref[0] = (acc_ref[...] * gate_ref[0]).astype(o_ref.dtype)


def expert_ffn(xe, gates, wg, wu, wd):
    E, C, D = xe.shape
    F = wg.shape[-1]
    tm = min(FFN_TOKEN_TILE, C)
    tf = min(FFN_HIDDEN_TILE, F)
    assert C % tm == 0 and F % tf == 0
    return pl.pallas_call(
        _ffn_kernel,
        grid=(E, C // tm, F // tf),
        in_specs=[
            pl.BlockSpec((1, tm, D), lambda e, i, f: (e, i, 0)),
            pl.BlockSpec((1, tm, 1), lambda e, i, f: (e, i, 0)),
            pl.BlockSpec((1, D, tf), lambda e, i, f: (e, 0, f)),
            pl.BlockSpec((1, D, tf), lambda e, i, f: (e, 0, f)),
            pl.BlockSpec((1, tf, D), lambda e, i, f: (e, f, 0)),
        ],
        out_specs=pl.BlockSpec((1, tm, D), lambda e, i, f: (e, i, 0)),
        out_shape=jax.ShapeDtypeStruct((E, C, D), jnp.bfloat16),
        scratch_shapes=[pltpu.VMEM((tm, D), jnp.float32)],
        compiler_params=_compiler_params(("parallel", "parallel", "arbitrary")),
        name="moe_ffn",
    )(xe, gates, wg, wu, wd)


ROUTE_CHUNK = LANES
ROUTE_WALKS = 8
COMBINE_TOKENS = 512
COMBINE_WINDOW = 128


def _route_select_kernel(aff_ref, tri_ref, ones_ref, low_ref, slot_ref, rank_ref, *, cap):
    a = aff_ref[0]

    def as_float(bits):
        return pltpu.bitcast(jnp.full((8, LANES), bits, jnp.int32), jnp.float32)[0:1, :]

    def bisect(_, lohi):
        lo, hi = lohi
        mid = lo + (hi - lo + 1) // 2
        ok = jnp.sum((a >= as_float(mid)).astype(jnp.int32)) >= cap
        return jnp.where(ok, mid, lo), jnp.where(ok, hi, mid - 1)

    thr_bits, _ = lax.fori_loop(0, 31, bisect, (jnp.int32(0), jnp.int32(0x7F800000)))
    thr = as_float(thr_bits)

    def exclusive_prefix(m):
        x = jnp.where(m, 1.0, 0.0).astype(jnp.bfloat16)
        incl = jnp.dot(x, tri_ref[...], preferred_element_type=jnp.float32)
        tot = jnp.dot(x, ones_ref[...], preferred_element_type=jnp.float32).astype(jnp.bfloat16)
        offs = jnp.dot(low_ref[...], tot, preferred_element_type=jnp.float32)
        return offs + incl - x.astype(jnp.float32)

    gt = a > thr
    eq = a == thr
    need = (cap - jnp.sum(gt.astype(jnp.int32))).astype(jnp.float32)
    sel = gt | (eq & (exclusive_prefix(eq) < need))
    rank = exclusive_prefix(sel).astype(jnp.int32)
    rank_ref[0] = rank
    slot_ref[0] = jnp.where(sel, rank, -1)


def route_select(aff3, cap):
    E, nc, w = aff3.shape
    tri = jnp.asarray(np.triu(np.ones((w, w), np.float32)), jnp.bfloat16)
    ones = jnp.ones((w, w), jnp.bfloat16)
    low = jnp.asarray(np.tril(np.ones((nc, nc), np.float32), -1), jnp.bfloat16)
    blk = pl.BlockSpec((1, nc, w), lambda e: (e, 0, 0))
    full = lambda a: pl.BlockSpec(a.shape, lambda e: (0,) * a.ndim)
    out = jax.ShapeDtypeStruct((E, nc, w), jnp.int32)
    return pl.pallas_call(
        partial(_route_select_kernel, cap=cap),
        grid=(E,),
        in_specs=[blk, full(tri), full(ones), full(low)],
        out_specs=[blk, blk],
        out_shape=[out, out],
        compiler_params=_compiler_params(("parallel",)),
        name="route_select",
    )(aff3, tri, ones, low)


def _route_compact_kernel(offs_ref, slot_ref, aff_ref, idx_ref, gate_ref, acc_ref, *, cap):
    e = pl.program_id(0)
    w = ROUTE_CHUNK
    nc = slot_ref.shape[1]
    span = nc // ROUTE_WALKS
    last_block = (cap - 1) // w
    sub = lax.broadcasted_iota(jnp.int32, (2 * w, w), 0)
    lane = lax.broadcasted_iota(jnp.int32, (1, w), 1)
    zero_row = jnp.zeros((1, w), jnp.float32)

    def block_of(c):
        return jnp.where(c < nc, jnp.minimum(offs_ref[e, jnp.minimum(c, nc - 1)] // w, last_block), last_block)

    def store(walk, block, rows):
        acc_ref[walk, 0, pl.ds(block, 1), :] = rows[0:1] * 256.0 + rows[1:2]
        acc_ref[walk, 1, pl.ds(block, 1), :] = rows[2:3] + rows[3:4] + rows[4:5]

    def chunk(walk, c, carry):
        b0 = block_of(c)
        rel = slot_ref[0, pl.ds(c, 1), :] - b0 * w
        onehot = jnp.where(sub == rel, 1.0, 0.0).astype(jnp.bfloat16)
        tok = c * w + lane
        g = aff_ref[0, pl.ds(c, 1), :]
        g_hi = g.astype(jnp.bfloat16).astype(jnp.float32)
        g_mid = (g - g_hi).astype(jnp.bfloat16).astype(jnp.float32)
        g_lo = g - g_hi - g_mid
        vals = jnp.concatenate([(tok >> 8).astype(jnp.float32), (tok & 255).astype(jnp.float32), g_hi, g_mid,
                                g_lo, zero_row, zero_row, zero_row], axis=0).astype(jnp.bfloat16)
        res = lax.dot_general(vals, onehot, (((1,), (1,)), ((), ())), preferred_element_type=jnp.float32)
        total = carry + res[:, :w]
        store(walk, b0, total)
        return jnp.where(block_of(c + 1) == b0, total, res[:, w:])

    acc_ref[...] = jnp.zeros_like(acc_ref)

    def step(j, carries):
        return tuple(chunk(q, q * span + j, carries[q]) for q in range(ROUTE_WALKS))

    zeros = jnp.zeros((8, w), jnp.float32)
    carries = lax.fori_loop(0, span, step, (zeros,) * ROUTE_WALKS, unroll=2)
    for q in range(ROUTE_WALKS):
        store(q, block_of((q + 1) * span), carries[q])
    idx_ref[0] = sum(acc_ref[q, 0] for q in range(ROUTE_WALKS)).astype(jnp.int32)
    gate_ref[0] = sum(acc_ref[q, 1] for q in range(ROUTE_WALKS))


def route_compact(slot3, aff3, offs, cap):
    E, nc, w = slot3.shape
    assert cap % w == 0 and nc % ROUTE_WALKS == 0
    blk = pl.BlockSpec((1, nc, w), lambda e, offs: (e, 0, 0))
    out_blk = pl.BlockSpec((1, cap // w, w), lambda e, offs: (e, 0, 0))
    idx, gates = pl.pallas_call(
        partial(_route_compact_kernel, cap=cap),
        grid_spec=pltpu.PrefetchScalarGridSpec(
            num_scalar_prefetch=1, grid=(E,), in_specs=[blk, blk], out_specs=[out_blk, out_blk],
            scratch_shapes=[pltpu.VMEM((ROUTE_WALKS, 2, cap // w, w), jnp.float32)]),
        out_shape=[jax.ShapeDtypeStruct((E, cap // w, w), jnp.int32),
                   jax.ShapeDtypeStruct((E, cap // w, w), jnp.float32)],
        compiler_params=_compiler_params(("arbitrary",)),
        name="route_compact",
    )(offs, slot3, aff3)
    return idx.reshape(E, cap), gates.reshape(E, cap)


def _combine_kernel(starts_ref, x_ref, slot_ref, g_ref, ye_hbm, o_ref, stage_ref, y_ref, sems, *, cap):
    j = pl.program_id(0)
    n_tiles = pl.num_programs(0)
    n_experts = slot_ref.shape[1]
    t = x_ref.shape[0]
    win = COMBINE_WINDOW
    buf = j % 2

    def window_copy(b, e, start):
        return pltpu.make_async_copy(ye_hbm.at[e, pl.ds(start, win), :], stage_ref.at[b, pl.ds(e * win, win), :],
                                     sems.at[b, e])

    def aligned(start):
        return pl.multiple_of(jnp.minimum((start // 16) * 16, cap - win), 16)

    def first_windows(tile):
        return [aligned(starts_ref[e, tile]) for e in range(n_experts)]

    @pl.when(j == 0)
    def _():
        for e, start in enumerate(first_windows(0)):
            window_copy(0, e, start).start()

    @pl.when(j + 1 < n_tiles)
    def _():
        for e, start in enumerate(first_windows(j + 1)):
            window_copy(1 - buf, e, start).start()

    bases = first_windows(j)
    lane = lax.broadcasted_iota(jnp.int32, (t, win), 1)
    slot = slot_ref[...]
    onehot = jnp.concatenate(
        [jnp.where(slot[:, e:e + 1] - bases[e] == lane, 1.0, 0.0).astype(jnp.bfloat16) for e in range(n_experts)],
        axis=1)
    for e in range(n_experts):
        window_copy(buf, e, bases[e]).wait()
    y_ref[...] = jnp.dot(onehot, stage_ref[buf], preferred_element_type=jnp.float32)

    for e in range(n_experts):
        end = starts_ref[e, j + 1]
        n_more = jnp.maximum(end - bases[e] + win - 1, 0) // win - 1

        def more(k, carry, e=e):
            first = bases[e] + (k + 1) * win
            start = aligned(first)
            cp = window_copy(buf, e, start)
            cp.start()
            cp.wait()
            col = slot_ref[:, e:e + 1]
            hit = (col - start == lane) & (col >= first)
            rows = stage_ref[buf, pl.ds(e * win, win), :]
            y_ref[...] += jnp.dot(jnp.where(hit, 1.0, 0.0).astype(jnp.bfloat16), rows,
                                  preferred_element_type=jnp.float32)
            return carry

        lax.fori_loop(0, n_more, more, 0)

    x = x_ref[...] + y_ref[...]
    o_ref[...] = x * _rms_scale(x) * g_ref[...]


def combine(x1, slot_t, starts, ye, g):
    N, D = x1.shape
    E, cap, _ = ye.shape
    t = min(COMBINE_TOKENS, N)
    assert N % t == 0 and cap % COMBINE_WINDOW == 0
    return pl.pallas_call(
        partial(_combine_kernel, cap=cap),
        grid_spec=pltpu.PrefetchScalarGridSpec(
            num_scalar_prefetch=1, grid=(N // t,),
            in_specs=[pl.BlockSpec((t, D), lambda j, s: (j, 0)), pl.BlockSpec((t, E), lambda j, s: (j, 0)),
                      pl.BlockSpec((1, D), lambda j, s: (0, 0)), pl.BlockSpec(memory_space=pl.ANY)],
            out_specs=pl.BlockSpec((t, D), lambda j, s: (j, 0)),
            scratch_shapes=[pltpu.VMEM((2, E * COMBINE_WINDOW, D), jnp.bfloat16), pltpu.VMEM((t, D), jnp.float32),
                            pltpu.SemaphoreType.DMA((2, E))]),
        out_shape=jax.ShapeDtypeStruct((N, D), jnp.float32),
        compiler_params=_compiler_params(("arbitrary",)),
        name="moe_combine",
    )(starts, x1, slot_t, g, ye)


DFT_P = 128
SLAB_PAD = 8
X_PITCH = DFT_P + SLAB_PAD
A_PITCH = 2 * DFT_P + SLAB_PAD
HY_SLABS = HYENA_WIDTH // LANES
U_SLABS = 3 * HY_SLABS
SLOW_STAGE_UNROLL = 128
SPECTRAL_UNROLL = 11


def _dft_sizes(M):
    q = M // DFT_P
    ka = q // 2 + 1
    kp = -(-ka // 8) * 8
    return q, ka, kp


def _dft_constants(M):
    p = DFT_P
    q, ka, kp = _dft_sizes(M)
    k = np.arange(ka)

    def slow_fwd(n_ts):
        ang = 2.0 * np.pi * ((k[:, None] * np.arange(n_ts)[None, :]) % q) / q
        m = np.zeros((2 * kp, n_ts))
        m[:ka] = np.cos(ang)
        m[kp:kp + ka] = -np.sin(ang)
        return m

    ts = np.arange(q // 2)
    ang = 2.0 * np.pi * ((ts[:, None] * k[None, :]) % q) / q
    c = np.where((k == 0) | (k == q // 2), 1.0, 2.0)[None, :]
    slow_inv = np.zeros((q // 2, 2 * kp))
    slow_inv[:, :ka] = c * np.cos(ang) / M
    slow_inv[:, kp:kp + ka] = -c * np.sin(ang) / M

    kb = np.arange(p)
    tf = np.arange(p)
    idx = (tf[None, None, :] * (k[:, None, None] + q * kb[None, :, None])) % M
    wr = np.cos(2.0 * np.pi * idx / M)
    wi = -np.sin(2.0 * np.pi * idx / M)
    w2 = np.concatenate([np.concatenate([wr, -wi], axis=2), np.concatenate([wi, wr], axis=2)], axis=1)
    bf = lambda a: jnp.asarray(a, jnp.float32).astype(jnp.bfloat16)
    return {"fa_half": bf(slow_fwd(q // 2)), "fa_full": bf(slow_fwd(q)), "fi": bf(slow_inv),
            "w2": bf(w2), "v2": bf(np.transpose(w2, (0, 2, 1)))}


def _slow_stage_forward(xs_ref, ag_ref, fa_ref, n_ts, kp):
    fa = fa_ref[...]

    def body(tf, carry):
        xt = xs_ref[pl.ds(tf, n_ts, stride=X_PITCH), :].astype(jnp.bfloat16)
        a = jnp.dot(fa, xt, preferred_element_type=jnp.float32)
        ag_ref[pl.ds(tf, kp, stride=A_PITCH), :] = a[:kp]
        ag_ref[pl.ds(DFT_P + tf, kp, stride=A_PITCH), :] = a[kp:]
        return carry

    lax.fori_loop(0, DFT_P, body, 0, unroll=SLOW_STAGE_UNROLL)


def _slow_stage_inverse(ag_ref, ys_ref, fi_ref, n_ts, kp):
    fi = fi_ref[...]

    def body(tf, carry):
        gr = ag_ref[pl.ds(tf, kp, stride=A_PITCH), :]
        gi = ag_ref[pl.ds(DFT_P + tf, kp, stride=A_PITCH), :]
        g = jnp.concatenate([gr, gi], axis=0).astype(jnp.bfloat16)
        ys_ref[pl.ds(tf, n_ts, stride=X_PITCH), :] = jnp.dot(fi, g, preferred_element_type=jnp.float32)
        return carry

    lax.fori_loop(0, DFT_P, body, 0, unroll=SLOW_STAGE_UNROLL)


def _fast_stage(ag_ref, w2_ref, k):
    r0 = pl.multiple_of(k * A_PITCH, 8)
    a2 = ag_ref[pl.ds(r0, 2 * DFT_P), :].astype(jnp.bfloat16)
    return r0, jnp.dot(w2_ref[k], a2, preferred_element_type=jnp.float32)


def _spectral_product(ag_ref, w2_ref, v2_ref, kspec_ref, ka):
    p = DFT_P

    def block(k):
        r0, x2 = _fast_stage(ag_ref, w2_ref, k)
        kk = kspec_ref[0, 0, pl.ds(pl.multiple_of(k * 2 * p, 2 * p), 2 * p), :]
        xr, xi, kr, ki = x2[:p], x2[p:], kk[:p], kk[p:]
        y2 = jnp.concatenate([xr * kr - xi * ki, xr * ki + xi * kr], axis=0).astype(jnp.bfloat16)
        return r0, jnp.dot(v2_ref[k], y2, preferred_element_type=jnp.float32)

    def body(j, carry):
        done = [block(SPECTRAL_UNROLL * j + i) for i in range(SPECTRAL_UNROLL)]
        for r0, g2 in done:
            ag_ref[pl.ds(r0, 2 * p), :] = g2
        return carry

    lax.fori_loop(0, ka // SPECTRAL_UNROLL, body, 0)
    rest = [block(k) for k in range(ka - ka % SPECTRAL_UNROLL, ka)]
    for r0, g2 in rest:
        ag_ref[pl.ds(r0, 2 * p), :] = g2


def _hyena_kernel(uv_ref, u1_ref, u2_ref, sw_ref, sb_ref, skip_ref, k0_ref, k1_ref, fa_ref, fi_ref, w2_ref,
                  v2_ref, o_ref, xs_ref, ys_ref, ag_ref, *, L):
    p = DFT_P
    c = pl.program_id(0)
    n_ts = L // p
    _, ka, kp = _dft_sizes(2 * L)
    row = lax.broadcasted_iota(jnp.int32, (p, LANES), 0)

    def short_conv_block(u_ref, group, ts):
        r0 = pl.multiple_of(ts * p, p)
        mid = u_ref[0, 0, pl.ds(r0, p), :]
        prev8 = u_ref[0, 0, pl.ds(pl.multiple_of(jnp.maximum(r0 - 8, 0), 8), 8), :]
        next8 = u_ref[0, 0, pl.ds(pl.multiple_of(jnp.minimum(r0 + p, L - 8), 8), 8), :]
        has_prev = (r0 > 0).astype(jnp.float32)
        has_next = (r0 + p < L).astype(jnp.float32)
        up = jnp.where(row == 0, prev8[7:8, :] * has_prev, pltpu.roll(mid, 1, 0))
        dn = jnp.where(row == p - 1, next8[0:1, :] * has_next, pltpu.roll(mid, p - 1, 0))
        ch = group * HY_SLABS + c
        w = lambda tap: sw_ref[pl.ds(tap * U_SLABS + ch, 1), :]
        return up * w(0) + mid * w(1) + dn * w(2) + sb_ref[pl.ds(ch, 1), :]

    def long_conv(kspec_ref):
        _slow_stage_forward(xs_ref, ag_ref, fa_ref, n_ts, kp)
        _spectral_product(ag_ref, w2_ref, v2_ref, kspec_ref, ka)
        _slow_stage_inverse(ag_ref, ys_ref, fi_ref, n_ts, kp)

    def blocks(fn):
        def body(ts, carry):
            fn(ts, pl.ds(pl.multiple_of(ts * X_PITCH, 8), p))
            return carry
        lax.fori_loop(0, n_ts, body, 0)

    def stage_v(ts, rows):
        xs_ref[rows, :] = short_conv_block(uv_ref, 0, ts)

    def stage_z(ts, rows):
        v = xs_ref[rows, :]
        y1 = ys_ref[rows, :] + skip_ref[pl.ds(c, 1), :] * v
        xs_ref[rows, :] = short_conv_block(u1_ref, 1, ts) * y1

    def stage_out(ts, rows):
        z = xs_ref[rows, :]
        y2 = ys_ref[rows, :] + skip_ref[pl.ds(HY_SLABS + c, 1), :] * z
        o_ref[0, 0, pl.ds(pl.multiple_of(ts * p, p), p), :] = short_conv_block(u2_ref, 2, ts) * y2

    blocks(stage_v)
    long_conv(k0_ref)
    blocks(stage_z)
    long_conv(k1_ref)
    blocks(stage_out)


def hyena_mix(u, sw, sb, skip, kspec, consts):
    B, n_slabs, L, _ = u.shape
    q, ka, kp = _dft_sizes(2 * L)
    n_ts = L // DFT_P
    u_spec = lambda g: pl.BlockSpec((1, 1, L, LANES), lambda c, b: (b, g * HY_SLABS + c, 0, 0))
    k_spec = lambda o: pl.BlockSpec((1, 1, ka * 2 * DFT_P, LANES), lambda c, b: (o, c, 0, 0),
                                    pipeline_mode=pl.Buffered(1))
    const = lambda a: pl.BlockSpec(a.shape, lambda c, b: (0,) * a.ndim, pipeline_mode=pl.Buffered(1))
    small = lambda a: pl.BlockSpec(a.shape, lambda c, b: (0,) * a.ndim)
    fa, fi, w2, v2 = consts["fa_half"], consts["fi"], consts["w2"], consts["v2"]
    return pl.pallas_call(
        partial(_hyena_kernel, L=L),
        grid=(HY_SLABS, B),
        in_specs=[u_spec(0), u_spec(1), u_spec(2), small(sw), small(sb), small(skip), k_spec(0), k_spec(1),
                  small(fa), small(fi), const(w2), const(v2)],
        out_specs=pl.BlockSpec((1, 1, L, LANES), lambda c, b: (b, c, 0, 0)),
        out_shape=jax.ShapeDtypeStruct((B, HY_SLABS, L, LANES), jnp.float32),
        scratch_shapes=[pltpu.VMEM((n_ts * X_PITCH, LANES), jnp.float32),
                        pltpu.VMEM((n_ts * X_PITCH, LANES), jnp.float32),
                        pltpu.VMEM((kp * A_PITCH, LANES), jnp.float32)],
        compiler_params=_compiler_params(("arbitrary", "arbitrary")),
        name="hyena_mix",
    )(u, u, u, sw, sb, skip, kspec, kspec, fa, fi, w2, v2)


def _filter_spectrum_kernel(kern_ref, fa_ref, w2_ref, o_ref, xs_ref, ag_ref, *, M):
    p = DFT_P
    q, ka, kp = _dft_sizes(M)

    def load(ts, carry):
        xs_ref[pl.ds(pl.multiple_of(ts * X_PITCH, 8), p), :] = kern_ref[0, 0, pl.ds(pl.multiple_of(ts * p, p), p), :]
        return carry

    lax.fori_loop(0, q, load, 0)
    _slow_stage_forward(xs_ref, ag_ref, fa_ref, q, kp)

    def body(k, carry):
        _, x2 = _fast_stage(ag_ref, w2_ref, k)
        o_ref[0, 0, pl.ds(pl.multiple_of(k * 2 * p, 2 * p), 2 * p), :] = x2
        return carry

    lax.fori_loop(0, ka, body, 0)


def filter_spectrum(kern, consts):
    n_o, n_c, M, _ = kern.shape
    q, ka, kp = _dft_sizes(M)
    fa, w2 = consts["fa_full"], consts["w2"]
    full = lambda a: pl.BlockSpec(a.shape, lambda o, c: (0,) * a.ndim)
    return pl.pallas_call(
        partial(_filter_spectrum_kernel, M=M),
        grid=(n_o, n_c),
        in_specs=[pl.BlockSpec((1, 1, M, LANES), lambda o, c: (o, c, 0, 0)), full(fa), full(w2)],
        out_specs=pl.BlockSpec((1, 1, ka * 2 * DFT_P, LANES), lambda o, c: (o, c, 0, 0)),
        out_shape=jax.ShapeDtypeStruct((n_o, n_c, ka * 2 * DFT_P, LANES), jnp.float32),
        scratch_shapes=[pltpu.VMEM((q * X_PITCH, LANES), jnp.float32),
                        pltpu.VMEM((kp * A_PITCH, LANES), jnp.float32)],
        compiler_params=_compiler_params(("arbitrary", "arbitrary")),
        name="filter_spectrum",
    )(kern, fa, w2)


FILTER_ROW_TILE = 512


def _filter_kernel(bands_ref, w1t_ref, w1c_ref, w1s_ref, b1_ref, w2_ref, b2_ref, w3_ref, b3_ref, w4_ref, freq_ref,
                   absd_ref, o_ref, *, L):
    tr = o_ref.shape[2]
    hp = lax.Precision.HIGHEST
    rho = pl.program_id(0) * tr + lax.broadcasted_iota(jnp.int32, (tr, 1), 0)
    tau = jnp.where(rho < L, rho, 2 * L - rho).astype(jnp.float32)
    t = tau * (1.0 / (L - 1))
    ang = (2.0 * math.pi * tau / L) * bands_ref[...]
    dot = lambda a, b: jnp.dot(a, b, preferred_element_type=jnp.float32, precision=hp)
    fr = freq_ref[...]
    pre = t * w1t_ref[...] + dot(jnp.cos(ang), w1c_ref[...]) - dot(jnp.sin(ang), w1s_ref[...]) + b1_ref[...]
    h = jnp.sin(fr * pre)
    h = jnp.sin(fr * (dot(h, w2_ref[...]) + b2_ref[...]))
    h = jnp.sin(fr * (dot(h, w3_ref[...]) + b3_ref[...]))
    hh = dot(h, w4_ref[...])
    decay = jnp.exp(-t * absd_ref[...])
    fwd = rho < L
    nonzero = rho != L
    w = HYENA_WIDTH
    for o in range(HYENA_ORDER):
        for c in range(HY_SLABS):
            f = hh[:, (2 * o) * w + c * LANES:(2 * o) * w + (c + 1) * LANES]
            b = hh[:, (2 * o + 1) * w + c * LANES:(2 * o + 1) * w + (c + 1) * LANES]
            val = jnp.where(fwd, f, b) * decay[:, c * LANES:(c + 1) * LANES]
            o_ref[o, c] = jnp.where(nonzero, val, 0.0)


def hyena_kernels(L, w1, b1, w2, b2, w3, b3, w4, freq):
    M = 2 * L
    tr = min(FILTER_ROW_TILE, M)
    bands = np.zeros((1, LANES), np.float32)
    bands[0, :FILTER_BANDS] = np.linspace(1e-4, FILTER_BANDS - 1, FILTER_BANDS, dtype=np.float32)
    pad_rows = lambda a: jnp.zeros((LANES, a.shape[1]), jnp.float32).at[:a.shape[0]].set(a)
    min_decay = math.log(DECAY_TARGET) / FAST_DECAY_PCT
    max_decay = math.log(DECAY_TARGET) / SLOW_DECAY_PCT
    absd = np.abs(np.linspace(min_decay, max_decay, HYENA_WIDTH, dtype=np.float32))[None, :]
    args = [jnp.asarray(bands), w1[0:1], pad_rows(w1[1:1 + FILTER_BANDS]), pad_rows(w1[1 + FILTER_BANDS:]),
            b1[None, :], w2, b2[None, :], w3, b3[None, :], w4, freq[None, :], jnp.asarray(absd)]
    full = lambda a: pl.BlockSpec(a.shape, lambda i: (0,) * a.ndim)
    return pl.pallas_call(
        partial(_filter_kernel, L=L),
        grid=(M // tr,),
        in_specs=[full(a) for a in args],
        out_specs=pl.BlockSpec((HYENA_ORDER, HY_SLABS, tr, LANES), lambda i: (0, 0, i, 0)),
        out_shape=jax.ShapeDtypeStruct((HYENA_ORDER, HY_SLABS, M, LANES), jnp.float32),
        compiler_params=_compiler_params(("parallel",)),
        name="hyena_filter",
    )(*args)


def hyena_operator(u, p):
    L = u.shape[2]
    consts = _dft_constants(2 * L)
    kern = hyena_kernels(L, p["filt_w1"], p["filt_b1"], p["filt_w2"], p["filt_b2"], p["filt_w3"], p["filt_b3"],
                         p["filt_w4"], p["filt_freq"])
    kspec = filter_spectrum(kern, consts)
    sw = p["short_w"].reshape(SHORT_CONV * U_SLABS, LANES)
    sb = p["short_b"].reshape(U_SLABS, LANES)
    skip = p["hyena_skip"].reshape(HYENA_ORDER * HY_SLABS, LANES)
    return hyena_mix(u, sw, sb, skip, kspec, consts)


def expert_choice_ffn(x1, h, affinity, p):
    N, D = h.shape
    E = affinity.shape[1]
    cap = EC_CAPACITY * N // N_EXPERTS
    aff3 = affinity.T.reshape(E, N // ROUTE_CHUNK, ROUTE_CHUNK)
    slot3, rank3 = route_select(aff3, cap)
    idx, gates = route_compact(slot3, aff3, rank3[:, :, 0], cap)
    ye = expert_ffn(h[idx], gates[..., None], p["w_gate"], p["w_up"], p["w_down"])
    t = min(COMBINE_TOKENS, N)
    starts = jnp.concatenate([rank3.reshape(E, N)[:, ::t], jnp.full((E, 1), cap, jnp.int32)], axis=1)
    return combine(x1, slot3.reshape(E, N).T, starts, ye, p["final_g"])


def encoder_trunk(x, p):
    B, L, D = x.shape
    qkv, u = in_proj(x, p["norm_mix_g"], p["w_in"])
    attn = dilated_attention(qkv)
    hy = hyena_operator(u, p)
    x1, h, affinity = out_proj(x, attn, hy, p["attn_out_g"], p["hyena_out_g"], p["w_out"], p["norm_ffn_g"],
                               p["w_router"])
    return expert_choice_ffn(x1, h, affinity, p).reshape(B, L, D)


def kernel(x_prompt, x_sample, norm_mix_g, w_in, short_w, short_b, filt_w1, filt_b1, filt_w2, filt_b2, filt_w3, filt_b3, filt_w4, filt_freq, hyena_skip, attn_out_g, hyena_out_g, w_out, norm_ffn_g, w_router, w_gate, w_up, w_down, final_g):
    bf16 = jnp.bfloat16
    p = {
        "norm_mix_g": norm_mix_g, "w_in": w_in[0].astype(bf16),
        "short_w": short_w[0], "short_b": short_b[0],
        "filt_w1": filt_w1[0], "filt_b1": filt_b1[0], "filt_w2": filt_w2[0], "filt_b2": filt_b2[0],
        "filt_w3": filt_w3[0], "filt_b3": filt_b3[0], "filt_w4": filt_w4[0], "filt_freq": filt_freq[0],
        "hyena_skip": hyena_skip[0], "attn_out_g": attn_out_g, "hyena_out_g": hyena_out_g,
        "w_out": w_out[0].astype(bf16), "norm_ffn_g": norm_ffn_g, "w_router": w_router[0],
        "w_gate": w_gate[0].astype(bf16), "w_up": w_up[0].astype(bf16), "w_down": w_down[0].astype(bf16),
        "final_g": final_g.reshape(1, D_MODEL),
    }
    return (encoder_trunk(x_prompt, p), encoder_trunk(x_sample, p))
```

```python
import math
from functools import partial

import jax
import jax.numpy as jnp
import numpy as np
from jax import lax
from jax.experimental import pallas as pl
from jax.experimental.pallas import tpu as pltpu

D_MODEL = 1024
HEAD_DIM = 64
N_ATTN_HEADS = 8
ATTN_WIDTH = N_ATTN_HEADS * HEAD_DIM
HYENA_WIDTH = D_MODEL - ATTN_WIDTH
HYENA_ORDER = 2
SHORT_CONV = 3
FILTER_EMB = 33
FILTER_BANDS = (FILTER_EMB - 1) // 2
FAST_DECAY_PCT = 0.3
SLOW_DECAY_PCT = 1.5
DECAY_TARGET = 1e-2
DILATED_PATTERNS = ((128, 1), (512, 4), (2048, 16))
N_EXPERTS = 16
EC_CAPACITY = 2
D_EXPERT = 2816
NORM_EPS = 1e-6
MASK_VALUE = -1e30

LANES = 128
VMEM_LIMIT_BYTES = 56 * 1024 * 1024

ROW_TILE = 512
FFN_TOKEN_TILE = 1024
FFN_HIDDEN_TILE = D_EXPERT // 2
ATTN_Q_TILE = 128
ATTN_TILES_PER_TRIP = 16


def _compiler_params(semantics):
    return pltpu.CompilerParams(dimension_semantics=semantics, vmem_limit_bytes=VMEM_LIMIT_BYTES)


def _rms_scale(x):
    return lax.rsqrt(jnp.mean(x * x, axis=-1, keepdims=True) + NORM_EPS)


def _in_proj_kernel(x_ref, g_ref, w_ref, qkv_ref, u_ref):
    x = x_ref[...]
    h = (x * _rms_scale(x) * g_ref[...]).astype(jnp.bfloat16)
    a = ATTN_WIDTH
    qkv = jnp.dot(h, w_ref[:, :3 * a], preferred_element_type=jnp.float32)
    for s in range(qkv_ref.shape[1]):
        cols = qkv[:, s * LANES:(s + 1) * LANES]
        qkv_ref[0, s] = cols * (HEAD_DIM ** -0.5) if s < a // LANES else cols
    u = jnp.dot(h, w_ref[:, 3 * a:], preferred_element_type=jnp.float32)
    for s in range(u_ref.shape[1]):
        u_ref[0, s] = u[:, s * LANES:(s + 1) * LANES]


def in_proj(x, g, w):
    B, L, D = x.shape
    N = B * L
    tm = min(ROW_TILE, L)
    assert L % tm == 0
    nt = L // tm
    a3, h3 = 3 * ATTN_WIDTH, 3 * HYENA_WIDTH
    slabs = lambda width: pl.BlockSpec((1, width // LANES, tm, LANES), lambda i: (i // nt, 0, i % nt, 0))
    return pl.pallas_call(
        _in_proj_kernel,
        grid=(N // tm,),
        in_specs=[
            pl.BlockSpec((tm, D), lambda i: (i, 0)),
            pl.BlockSpec((1, D), lambda i: (0, 0)),
            pl.BlockSpec((D, a3 + h3), lambda i: (0, 0)),
        ],
        out_specs=[slabs(a3), slabs(h3)],
        out_shape=[jax.ShapeDtypeStruct((B, a3 // LANES, L, LANES), jnp.float32),
                   jax.ShapeDtypeStruct((B, h3 // LANES, L, LANES), jnp.float32)],
        compiler_params=_compiler_params(("parallel",)),
        name="in_proj",
    )(x.reshape(N, D), g, w)


def _alibi_slope(h):
    return 2.0 ** (-8.0 * (h + 1) / N_ATTN_HEADS)


def _attn_tiling(L, window, dilation):
    radius = window // (2 * dilation)
    n = L // dilation
    tq = min(ATTN_Q_TILE, n)
    tk = min(tq + 2 * radius, n)
    return radius, n, tq, tk


def _attn_kernel(q_ref, k_ref, v_ref, o_ref, ob_ref, lse_ref, *bias_refs, L, hp_axis):
    hp = pl.program_id(hp_axis)

    def pick(values):
        out = jnp.float32(values[-1])
        for j in range(len(values) - 2, -1, -1):
            out = jnp.where(hp == j, jnp.float32(values[j]), out)
        return out

    n_pairs = ATTN_WIDTH // LANES
    slope_a = pick([_alibi_slope(2 * j) for j in range(n_pairs)])
    slope_b = pick([_alibi_slope(2 * j + 1) for j in range(n_pairs)])

    for branch, (window, dilation) in enumerate(DILATED_PATTERNS):
        radius, n, tq, tk = _attn_tiling(L, window, dilation)
        tiles = n // tq

        lane = lax.broadcasted_iota(jnp.int32, (tq, LANES), 1)
        low = lane < HEAD_DIM
        row2 = lax.broadcasted_iota(jnp.int32, (2 * tq, tk), 0)
        col2 = lax.broadcasted_iota(jnp.int32, (2 * tq, tk), 1)
        base_rel = col2 - jnp.where(row2 >= tq, row2 - tq, row2)
        second_col = lax.broadcasted_iota(jnp.int32, (2 * tq, 1), 0) >= tq
        slope = jnp.where(second_col, slope_b, slope_a) * float(dilation)

        offsets = {q0 - min(max(q0 - radius, 0), n - tk) for q0 in range(0, n, tq)}
        assert offsets <= {0, radius, 2 * radius}
        bias_ref = bias_refs[branch]
        for case in range(3):
            absrel = jnp.abs(base_rel - case * radius)
            bias_ref[case] = jnp.where(absrel <= radius, -(slope * absrel.astype(jnp.float32)), MASK_VALUE)

        def tile(it, carry, dilation=dilation, radius=radius, n=n, tq=tq, tk=tk, tiles=tiles,
                 low=low, lane=lane, bias_ref=bias_ref):
            r = it // tiles
            q0 = (it % tiles) * tq
            ks = jnp.clip(q0 - radius, 0, n - tk)
            q_rows = pl.ds(r + dilation * q0, tq, stride=dilation)
            k_rows = pl.ds(r + dilation * ks, tk, stride=dilation)
            q = q_ref[0, 0, q_rows, :].astype(jnp.bfloat16)
            kw = k_ref[0, 0, k_rows, :].astype(jnp.bfloat16)
            vw = v_ref[0, 0, k_rows, :].astype(jnp.bfloat16)
            zero = jnp.zeros_like(q)
            q2 = jnp.concatenate([jnp.where(low, q, zero), jnp.where(low, zero, q)], axis=0)
            s = lax.dot_general(q2, kw, (((1,), (1,)), ((), ())), preferred_element_type=jnp.float32)
            s = s + bias_ref[(q0 - ks) // radius]
            m = jnp.max(s, axis=-1, keepdims=True)
            p = jnp.exp(s - m)
            l = jnp.sum(p, axis=-1, keepdims=True)
            o2 = jnp.dot(p.astype(jnp.bfloat16), vw, preferred_element_type=jnp.float32) / l
            lse2 = m + jnp.log(l)
            o = jnp.where(low, o2[:tq], o2[tq:])
            lse = jnp.where(lane == 0, lse2[:tq], jnp.where(lane == 1, lse2[tq:], 0.0))
            return q_rows, o, lse

        per_trip = min(ATTN_TILES_PER_TRIP, dilation * tiles)

        def tiles_body(j, carry, tile=tile, per_trip=per_trip, branch=branch):
            done = [tile(per_trip * j + i, carry) for i in range(per_trip)]
            for q_rows, o, lse in done:
                ob_ref[branch, q_rows, :] = o
                lse_ref[branch, q_rows, :] = lse
            return carry

        assert (dilation * tiles) % per_trip == 0
        lax.fori_loop(0, dilation * tiles // per_trip, tiles_body, 0)

    n_branches = len(DILATED_PATTERNS)
    tm = min(ATTN_Q_TILE, L)
    src = lax.broadcasted_iota(jnp.int32, (LANES, LANES), 0)
    dst = lax.broadcasted_iota(jnp.int32, (LANES, LANES), 1)
    spread = jnp.where(src * HEAD_DIM <= dst, jnp.where(dst < (src + 1) * HEAD_DIM, 1.0, 0.0), 0.0)
    spread = spread.astype(jnp.bfloat16)

    def mix(i, carry):
        rows = pl.ds(pl.multiple_of(i * tm, tm), tm)
        lses = [lse_ref[b, rows, :] for b in range(n_branches)]
        mx = lses[0]
        for x in lses[1:]:
            mx = jnp.maximum(mx, x)
        es = [jnp.exp(x - mx) for x in lses]
        inv = 1.0 / sum(es)
        w = jnp.concatenate([e * inv for e in es], axis=0)
        w_hi = w.astype(jnp.bfloat16)
        w_lo = (w - w_hi.astype(jnp.float32)).astype(jnp.bfloat16)
        wide = (jnp.dot(w_hi, spread, preferred_element_type=jnp.float32)
                + jnp.dot(w_lo, spread, preferred_element_type=jnp.float32))
        o = 0.0
        for b in range(n_branches):
            o = o + ob_ref[b, rows, :] * wide[b * tm:(b + 1) * tm]
        o_ref[0, 0, rows, :] = o
        return carry

    lax.fori_loop(0, L // tm, mix, 0, unroll=4)


def dilated_attention(qkv):
    B, _, L, _ = qkv.shape
    n_pairs = ATTN_WIDTH // LANES
    bias_scratch = []
    for window, dilation in DILATED_PATTERNS:
        radius, n, tq, tk = _attn_tiling(L, window, dilation)
        assert L % dilation == 0 and n % tq == 0
        bias_scratch.append(pltpu.VMEM((3, 2 * tq, tk), jnp.float32))
    branch_state = pltpu.VMEM((len(DILATED_PATTERNS), L, LANES), jnp.float32)
    spec = lambda g: pl.BlockSpec((1, 1, L, LANES), lambda b, hp: (b, g * n_pairs + hp, 0, 0))
    return pl.pallas_call(
        partial(_attn_kernel, L=L, hp_axis=1),
        grid=(B, n_pairs),
        in_specs=[spec(0), spec(1), spec(2)],
        out_specs=pl.BlockSpec((1, 1, L, LANES), lambda b, hp: (b, hp, 0, 0)),
        out_shape=jax.ShapeDtypeStruct((B, n_pairs, L, LANES), jnp.float32),
        scratch_shapes=[branch_state, branch_state] + bias_scratch,
        compiler_params=_compiler_params(("parallel", "arbitrary")),
        name="dilated_attn",
    )(qkv, qkv, qkv)


def _out_proj_kernel(x_ref, attn_ref, hy_ref, ag_ref, hg_ref, w_ref, ng_ref, wr_ref, x1_ref, h_ref, aff_ref):
    a = ATTN_WIDTH

    def group_norm(ref, g_ref):
        v = jnp.concatenate([ref[0, s] for s in range(ref.shape[1])], axis=-1)
        return (v * _rms_scale(v) * g_ref[...]).astype(jnp.bfloat16)

    y = jnp.dot(group_norm(attn_ref, ag_ref), w_ref[:a, :], preferred_element_type=jnp.float32)
    y = y + jnp.dot(group_norm(hy_ref, hg_ref), w_ref[a:, :], preferred_element_type=jnp.float32)
    x1 = x_ref[...] + y
    x1_ref[...] = x1
    h = x1 * _rms_scale(x1) * ng_ref[...]
    h_hi = h.astype(jnp.bfloat16)
    h_ref[...] = h_hi
    h_lo = (h - h_hi.astype(jnp.float32)).astype(jnp.bfloat16)
    wr = wr_ref[...]
    wr_hi = wr.astype(jnp.bfloat16)
    wr_lo = (wr - wr_hi.astype(jnp.float32)).astype(jnp.bfloat16)
    dot = lambda p, q: jnp.dot(p, q, preferred_element_type=jnp.float32)
    logits = dot(h_hi, wr_hi) + (dot(h_lo, wr_hi) + dot(h_hi, wr_lo))
    e = jnp.exp(logits - jnp.max(logits, axis=-1, keepdims=True))
    aff_ref[...] = e / jnp.sum(e, axis=-1, keepdims=True)


def out_proj(x, attn, hy, attn_g, hy_g, w, ffn_g, w_router):
    B, L, D = x.shape
    N = B * L
    tm = min(ROW_TILE, L)
    assert L % tm == 0
    nt = L // tm
    E = w_router.shape[-1]
    slabs = lambda width: pl.BlockSpec((1, width // LANES, tm, LANES), lambda i: (i // nt, 0, i % nt, 0))
    row = lambda width: pl.BlockSpec((tm, width), lambda i: (i, 0))
    full = lambda r, c: pl.BlockSpec((r, c), lambda i: (0, 0))
    return pl.pallas_call(
        _out_proj_kernel,
        grid=(N // tm,),
        in_specs=[row(D), slabs(ATTN_WIDTH), slabs(HYENA_WIDTH), full(1, ATTN_WIDTH), full(1, HYENA_WIDTH),
                  full(D, D), full(1, D), full(D, E)],
        out_specs=[row(D), row(D), row(E)],
        out_shape=[jax.ShapeDtypeStruct((N, D), jnp.float32), jax.ShapeDtypeStruct((N, D), jnp.bfloat16),
                   jax.ShapeDtypeStruct((N, E), jnp.float32)],
        compiler_params=_compiler_params(("parallel",)),
        name="out_proj",
    )(x.reshape(N, D), attn, hy, attn_g, hy_g, w, ffn_g, w_router)


def _ffn_kernel(x_ref, gate_ref, wg_ref, wu_ref, wd_ref, o_ref, acc_ref):
    f = pl.program_id(2)
    x = x_ref[0]
    a = jnp.dot(x, wg_ref[0], preferred_element_type=jnp.float32)
    b = jnp.dot(x, wu_ref[0], preferred_element_type=jnp.float32)
    h = (a * jax.nn.sigmoid(a) * b).astype(jnp.bfloat16)
    y = jnp.dot(h, wd_ref[0], preferred_element_type=jnp.float32)

    @pl.when(f == 0)
    def _():
        acc_ref[...] = y

    @pl.when(f > 0)
    def _():
        acc_ref[...] += y

    @pl.when(f == pl.num_programs(2) - 1)
    def _():
        o_ref[0] = (acc_ref[...] * gate_ref[0]).astype(o_ref.dtype)


def expert_ffn(xe, gates, wg, wu, wd):
    E, C, D = xe.shape
    F = wg.shape[-1]
    tm = min(FFN_TOKEN_TILE, C)
    tf = min(FFN_HIDDEN_TILE, F)
    assert C % tm == 0 and F % tf == 0
    return pl.pallas_call(
        _ffn_kernel,
        grid=(E, C // tm, F // tf),
        in_specs=[
            pl.BlockSpec((1, tm, D), lambda e, i, f: (e, i, 0)),
            pl.BlockSpec((1, tm, 1), lambda e, i, f: (e, i, 0)),
            pl.BlockSpec((1, D, tf), lambda e, i, f: (e, 0, f)),
            pl.BlockSpec((1, D, tf), lambda e, i, f: (e, 0, f)),
            pl.BlockSpec((1, tf, D), lambda e, i, f: (e, f, 0)),
        ],
        out_specs=pl.BlockSpec((1, tm, D), lambda e, i, f: (e, i, 0)),
        out_shape=jax.ShapeDtypeStruct((E, C, D), jnp.bfloat16),
        scratch_shapes=[pltpu.VMEM((tm, D), jnp.float32)],
        compiler_params=_compiler_params(("parallel", "parallel", "arbitrary")),
        name="moe_ffn",
    )(xe, gates, wg, wu, wd)


ROUTE_CHUNK = LANES
ROUTE_WALKS = 8
COMBINE_TOKENS = 512
COMBINE_WINDOW = 128


def _route_select_kernel(aff_ref, tri_ref, ones_ref, low_ref, slot_ref, rank_ref, *, cap):
    a = aff_ref[0]

    def as_float(bits):
        return pltpu.bitcast(jnp.full((8, LANES), bits, jnp.int32), jnp.float32)[0:1, :]

    def bisect(_, lohi):
        lo, hi = lohi
        mid = lo + (hi - lo + 1) // 2
        ok = jnp.sum((a >= as_float(mid)).astype(jnp.int32)) >= cap
        return jnp.where(ok, mid, lo), jnp.where(ok, hi, mid - 1)

    thr_bits, _ = lax.fori_loop(0, 31, bisect, (jnp.int32(0), jnp.int32(0x7F800000)))
    thr = as_float(thr_bits)

    def exclusive_prefix(m):
        x = jnp.where(m, 1.0, 0.0).astype(jnp.bfloat16)
        incl = jnp.dot(x, tri_ref[...], preferred_element_type=jnp.float32)
        tot = jnp.dot(x, ones_ref[...], preferred_element_type=jnp.float32).astype(jnp.bfloat16)
        offs = jnp.dot(low_ref[...], tot, preferred_element_type=jnp.float32)
        return offs + incl - x.astype(jnp.float32)

    gt = a > thr
    eq = a == thr
    need = (cap - jnp.sum(gt.astype(jnp.int32))).astype(jnp.float32)
    sel = gt | (eq & (exclusive_prefix(eq) < need))
    rank = exclusive_prefix(sel).astype(jnp.int32)
    rank_ref[0] = rank
    slot_ref[0] = jnp.where(sel, rank, -1)


def route_select(aff3, cap):
    E, nc, w = aff3.shape
    tri = jnp.asarray(np.triu(np.ones((w, w), np.float32)), jnp.bfloat16)
    ones = jnp.ones((w, w), jnp.bfloat16)
    low = jnp.asarray(np.tril(np.ones((nc, nc), np.float32), -1), jnp.bfloat16)
    blk = pl.BlockSpec((1, nc, w), lambda e: (e, 0, 0))
    full = lambda a: pl.BlockSpec(a.shape, lambda e: (0,) * a.ndim)
    out = jax.ShapeDtypeStruct((E, nc, w), jnp.int32)
    return pl.pallas_call(
        partial(_route_select_kernel, cap=cap),
        grid=(E,),
        in_specs=[blk, full(tri), full(ones), full(low)],
        out_specs=[blk, blk],
        out_shape=[out, out],
        compiler_params=_compiler_params(("parallel",)),
        name="route_select",
    )(aff3, tri, ones, low)


def _route_compact_kernel(offs_ref, slot_ref, aff_ref, idx_ref, gate_ref, acc_ref, *, cap):
    e = pl.program_id(0)
    w = ROUTE_CHUNK
    nc = slot_ref.shape[1]
    span = nc // ROUTE_WALKS
    last_block = (cap - 1) // w
    sub = lax.broadcasted_iota(jnp.int32, (2 * w, w), 0)
    lane = lax.broadcasted_iota(jnp.int32, (1, w), 1)
    zero_row = jnp.zeros((1, w), jnp.float32)

    def block_of(c):
        return jnp.where(c < nc, jnp.minimum(offs_ref[e, jnp.minimum(c, nc - 1)] // w, last_block), last_block)

    def store(walk, block, rows):
        acc_ref[walk, 0, pl.ds(block, 1), :] = rows[0:1] * 256.0 + rows[1:2]
        acc_ref[walk, 1, pl.ds(block, 1), :] = rows[2:3] + rows[3:4] + rows[4:5]

    def chunk(walk, c, carry):
        b0 = block_of(c)
        rel = slot_ref[0, pl.ds(c, 1), :] - b0 * w
        onehot = jnp.where(sub == rel, 1.0, 0.0).astype(jnp.bfloat16)
        tok = c * w + lane
        g = aff_ref[0, pl.ds(c, 1), :]
        g_hi = g.astype(jnp.bfloat16).astype(jnp.float32)
        g_mid = (g - g_hi).astype(jnp.bfloat16).astype(jnp.float32)
        g_lo = g - g_hi - g_mid
        vals = jnp.concatenate([(tok >> 8).astype(jnp.float32), (tok & 255).astype(jnp.float32), g_hi, g_mid,
                                g_lo, zero_row, zero_row, zero_row], axis=0).astype(jnp.bfloat16)
        res = lax.dot_general(vals, onehot, (((1,), (1,)), ((), ())), preferred_element_type=jnp.float32)
        total = carry + res[:, :w]
        store(walk, b0, total)
        return jnp.where(block_of(c + 1) == b0, total, res[:, w:])

    acc_ref[...] = jnp.zeros_like(acc_ref)

    def step(j, carries):
        return tuple(chunk(q, q * span + j, carries[q]) for q in range(ROUTE_WALKS))

    zeros = jnp.zeros((8, w), jnp.float32)
    carries = lax.fori_loop(0, span, step, (zeros,) * ROUTE_WALKS, unroll=2)
    for q in range(ROUTE_WALKS):
        store(q, block_of((q + 1) * span), carries[q])
    idx_ref[0] = sum(acc_ref[q, 0] for q in range(ROUTE_WALKS)).astype(jnp.int32)
    gate_ref[0] = sum(acc_ref[q, 1] for q in range(ROUTE_WALKS))


def route_compact(slot3, aff3, offs, cap):
    E, nc, w = slot3.shape
    assert cap % w == 0 and nc % ROUTE_WALKS == 0
    blk = pl.BlockSpec((1, nc, w), lambda e, offs: (e, 0, 0))
    out_blk = pl.BlockSpec((1, cap // w, w), lambda e, offs: (e, 0, 0))
    idx, gates = pl.pallas_call(
        partial(_route_compact_kernel, cap=cap),
        grid_spec=pltpu.PrefetchScalarGridSpec(
            num_scalar_prefetch=1, grid=(E,), in_specs=[blk, blk], out_specs=[out_blk, out_blk],
            scratch_shapes=[pltpu.VMEM((ROUTE_WALKS, 2, cap // w, w), jnp.float32)]),
        out_shape=[jax.ShapeDtypeStruct((E, cap // w, w), jnp.int32),
                   jax.ShapeDtypeStruct((E, cap // w, w), jnp.float32)],
        compiler_params=_compiler_params(("arbitrary",)),
        name="route_compact",
    )(offs, slot3, aff3)
    return idx.reshape(E, cap), gates.reshape(E, cap)


def _combine_kernel(starts_ref, x_ref, slot_ref, g_ref, ye_hbm, o_ref, stage_ref, y_ref, sems, *, cap):
    j = pl.program_id(0)
    n_tiles = pl.num_programs(0)
    n_experts = slot_ref.shape[1]
    t = x_ref.shape[0]
    win = COMBINE_WINDOW
    buf = j % 2

    def window_copy(b, e, start):
        return pltpu.make_async_copy(ye_hbm.at[e, pl.ds(start, win), :], stage_ref.at[b, pl.ds(e * win, win), :],
                                     sems.at[b, e])

    def aligned(start):
        return pl.multiple_of(jnp.minimum((start // 16) * 16, cap - win), 16)

    def first_windows(tile):
        return [aligned(starts_ref[e, tile]) for e in range(n_experts)]

    @pl.when(j == 0)
    def _():
        for e, start in enumerate(first_windows(0)):
            window_copy(0, e, start).start()

    @pl.when(j + 1 < n_tiles)
    def _():
        for e, start in enumerate(first_windows(j + 1)):
            window_copy(1 - buf, e, start).start()

    bases = first_windows(j)
    lane = lax.broadcasted_iota(jnp.int32, (t, win), 1)
    slot = slot_ref[...]
    onehot = jnp.concatenate(
        [jnp.where(slot[:, e:e + 1] - bases[e] == lane, 1.0, 0.0).astype(jnp.bfloat16) for e in range(n_experts)],
        axis=1)
    for e in range(n_experts):
        window_copy(buf, e, bases[e]).wait()
    y_ref[...] = jnp.dot(onehot, stage_ref[buf], preferred_element_type=jnp.float32)

    for e in range(n_experts):
        end = starts_ref[e, j + 1]
        n_more = jnp.maximum(end - bases[e] + win - 1, 0) // win - 1

        def more(k, carry, e=e):
            first = bases[e] + (k + 1) * win
            start = aligned(first)
            cp = window_copy(buf, e, start)
            cp.start()
            cp.wait()
            col = slot_ref[:, e:e + 1]
            hit = (col - start == lane) & (col >= first)
            rows = stage_ref[buf, pl.ds(e * win, win), :]
            y_ref[...] += jnp.dot(jnp.where(hit, 1.0, 0.0).astype(jnp.bfloat16), rows,
                                  preferred_element_type=jnp.float32)
            return carry

        lax.fori_loop(0, n_more, more, 0)

    x = x_ref[...] + y_ref[...]
    o_ref[...] = x * _rms_scale(x) * g_ref[...]


def combine(x1, slot_t, starts, ye, g):
    N, D = x1.shape
    E, cap, _ = ye.shape
    t = min(COMBINE_TOKENS, N)
    assert N % t == 0 and cap % COMBINE_WINDOW == 0
    return pl.pallas_call(
        partial(_combine_kernel, cap=cap),
        grid_spec=pltpu.PrefetchScalarGridSpec(
            num_scalar_prefetch=1, grid=(N // t,),
            in_specs=[pl.BlockSpec((t, D), lambda j, s: (j, 0)), pl.BlockSpec((t, E), lambda j, s: (j, 0)),
                      pl.BlockSpec((1, D), lambda j, s: (0, 0)), pl.BlockSpec(memory_space=pl.ANY)],
            out_specs=pl.BlockSpec((t, D), lambda j, s: (j, 0)),
            scratch_shapes=[pltpu.VMEM((2, E * COMBINE_WINDOW, D), jnp.bfloat16), pltpu.VMEM((t, D), jnp.float32),
                            pltpu.SemaphoreType.DMA((2, E))]),
        out_shape=jax.ShapeDtypeStruct((N, D), jnp.float32),
        compiler_params=_compiler_params(("arbitrary",)),
        name="moe_combine",
    )(starts, x1, slot_t, g, ye)


DFT_P = 128
SLAB_PAD = 8
X_PITCH = DFT_P + SLAB_PAD
A_PITCH = 2 * DFT_P + SLAB_PAD
HY_SLABS = HYENA_WIDTH // LANES
U_SLABS = 3 * HY_SLABS
SLOW_STAGE_UNROLL = 128
SPECTRAL_UNROLL = 33


def _dft_sizes(M):
    q = M // DFT_P
    ka = q // 2 + 1
    kp = -(-ka // 8) * 8
    return q, ka, kp


def _dft_constants(M):
    p = DFT_P
    q, ka, kp = _dft_sizes(M)
    k = np.arange(ka)

    def slow_fwd(n_ts):
        ang = 2.0 * np.pi * ((k[:, None] * np.arange(n_ts)[None, :]) % q) / q
        m = np.zeros((2 * kp, n_ts))
        m[:ka] = np.cos(ang)
        m[kp:kp + ka] = -np.sin(ang)
        return m

    ts = np.arange(q // 2)
    ang = 2.0 * np.pi * ((ts[:, None] * k[None, :]) % q) / q
    c = np.where((k == 0) | (k == q // 2), 1.0, 2.0)[None, :]
    slow_inv = np.zeros((q // 2, 2 * kp))
    slow_inv[:, :ka] = c * np.cos(ang) / M
    slow_inv[:, kp:kp + ka] = -c * np.sin(ang) / M

    kb = np.arange(p)
    tf = np.arange(p)
    idx = (tf[None, None, :] * (k[:, None, None] + q * kb[None, :, None])) % M
    wr = np.cos(2.0 * np.pi * idx / M)
    wi = -np.sin(2.0 * np.pi * idx / M)
    w2 = np.concatenate([np.concatenate([wr, -wi], axis=2), np.concatenate([wi, wr], axis=2)], axis=1)
    bf = lambda a: jnp.asarray(a, jnp.float32).astype(jnp.bfloat16)
    return {"fa_half": bf(slow_fwd(q // 2)), "fa_full": bf(slow_fwd(q)), "fi": bf(slow_inv),
            "w2": bf(w2), "v2": bf(np.transpose(w2, (0, 2, 1)))}


def _slow_stage_forward(xs_ref, ag_ref, fa_ref, n_ts, kp):
    fa = fa_ref[...]

    def body(tf, carry):
        xt = xs_ref[pl.ds(tf, n_ts, stride=X_PITCH), :].astype(jnp.bfloat16)
        a = jnp.dot(fa, xt, preferred_element_type=jnp.float32)
        ag_ref[pl.ds(tf, kp, stride=A_PITCH), :] = a[:kp]
        ag_ref[pl.ds(DFT_P + tf, kp, stride=A_PITCH), :] = a[kp:]
        return carry

    lax.fori_loop(0, DFT_P, body, 0, unroll=SLOW_STAGE_UNROLL)


def _slow_stage_inverse(ag_ref, ys_ref, fi_ref, n_ts, kp):
    fi = fi_ref[...]

    def body(tf, carry):
        gr = ag_ref[pl.ds(tf, kp, stride=A_PITCH), :]
        gi = ag_ref[pl.ds(DFT_P + tf, kp, stride=A_PITCH), :]
        g = jnp.concatenate([gr, gi], axis=0).astype(jnp.bfloat16)
        ys_ref[pl.ds(tf, n_ts, stride=X_PITCH), :] = jnp.dot(fi, g, preferred_element_type=jnp.float32)
        return carry

    lax.fori_loop(0, DFT_P, body, 0, unroll=SLOW_STAGE_UNROLL)


def _fast_stage(ag_ref, w2_ref, k):
    r0 = pl.multiple_of(k * A_PITCH, 8)
    a2 = ag_ref[pl.ds(r0, 2 * DFT_P), :].astype(jnp.bfloat16)
    return r0, jnp.dot(w2_ref[k], a2, preferred_element_type=jnp.float32)


def _spectral_product(ag_ref, w2_ref, v2_ref, kspec_ref, ka):
    p = DFT_P

    def block(k):
        r0, x2 = _fast_stage(ag_ref, w2_ref, k)
        kk = kspec_ref[0, 0, pl.ds(pl.multiple_of(k * 2 * p, 2 * p), 2 * p), :]
        xr, xi, kr, ki = x2[:p], x2[p:], kk[:p], kk[p:]
        y2 = jnp.concatenate([xr * kr - xi * ki, xr * ki + xi * kr], axis=0).astype(jnp.bfloat16)
        return r0, jnp.dot(v2_ref[k], y2, preferred_element_type=jnp.float32)

    def body(j, carry):
        done = [block(SPECTRAL_UNROLL * j + i) for i in range(SPECTRAL_UNROLL)]
        for r0, g2 in done:
            ag_ref[pl.ds(r0, 2 * p), :] = g2
        return carry

    lax.fori_loop(0, ka // SPECTRAL_UNROLL, body, 0)
    rest = [block(k) for k in range(ka - ka % SPECTRAL_UNROLL, ka)]
    for r0, g2 in rest:
        ag_ref[pl.ds(r0, 2 * p), :] = g2


def _hyena_kernel(uv_ref, u1_ref, u2_ref, sw_ref, sb_ref, skip_ref, k0_ref, k1_ref, fa_ref, fi_ref, w2_ref,
                  v2_ref, o_ref, xs_ref, ys_ref, ag_ref, *, L):
    p = DFT_P
    c = pl.program_id(0)
    n_ts = L // p
    _, ka, kp = _dft_sizes(2 * L)
    row = lax.broadcasted_iota(jnp.int32, (p, LANES), 0)

    def short_conv_block(u_ref, group, ts):
        r0 = pl.multiple_of(ts * p, p)
        mid = u_ref[0, 0, pl.ds(r0, p), :]
        prev8 = u_ref[0, 0, pl.ds(pl.multiple_of(jnp.maximum(r0 - 8, 0), 8), 8), :]
        next8 = u_ref[0, 0, pl.ds(pl.multiple_of(jnp.minimum(r0 + p, L - 8), 8), 8), :]
        has_prev = (r0 > 0).astype(jnp.float32)
        has_next = (r0 + p < L).astype(jnp.float32)
        up = jnp.where(row == 0, prev8[7:8, :] * has_prev, pltpu.roll(mid, 1, 0))
        dn = jnp.where(row == p - 1, next8[0:1, :] * has_next, pltpu.roll(mid, p - 1, 0))
        ch = group * HY_SLABS + c
        w = lambda tap: sw_ref[pl.ds(tap * U_SLABS + ch, 1), :]
        return up * w(0) + mid * w(1) + dn * w(2) + sb_ref[pl.ds(ch, 1), :]

    def long_conv(kspec_ref):
        _slow_stage_forward(xs_ref, ag_ref, fa_ref, n_ts, kp)
        _spectral_product(ag_ref, w2_ref, v2_ref, kspec_ref, ka)
        _slow_stage_inverse(ag_ref, ys_ref, fi_ref, n_ts, kp)

    def blocks(fn):
        def body(ts, carry):
            fn(ts, pl.ds(pl.multiple_of(ts * X_PITCH, 8), p))
            return carry
        lax.fori_loop(0, n_ts, body, 0)

    def stage_v(ts, rows):
        xs_ref[rows, :] = short_conv_block(uv_ref, 0, ts)

    def stage_z(ts, rows):
        v = xs_ref[rows, :]
        y1 = ys_ref[rows, :] + skip_ref[pl.ds(c, 1), :] * v
        xs_ref[rows, :] = short_conv_block(u1_ref, 1, ts) * y1

    def stage_out(ts, rows):
        z = xs_ref[rows, :]
        y2 = ys_ref[rows, :] + skip_ref[pl.ds(HY_SLABS + c, 1), :] * z
        o_ref[0, 0, pl.ds(pl.multiple_of(ts * p, p), p), :] = short_conv_block(u2_ref, 2, ts) * y2

    blocks(stage_v)
    long_conv(k0_ref)
    blocks(stage_z)
    long_conv(k1_ref)
    blocks(stage_out)


def hyena_mix(u, sw, sb, skip, kspec, consts):
    B, n_slabs, L, _ = u.shape
    q, ka, kp = _dft_sizes(2 * L)
    n_ts = L // DFT_P
    u_spec = lambda g: pl.BlockSpec((1, 1, L, LANES), lambda c, b: (b, g * HY_SLABS + c, 0, 0))
    k_spec = lambda o: pl.BlockSpec((1, 1, ka * 2 * DFT_P, LANES), lambda c, b: (o, c, 0, 0),
                                    pipeline_mode=pl.Buffered(1))
    const = lambda a: pl.BlockSpec(a.shape, lambda c, b: (0,) * a.ndim, pipeline_mode=pl.Buffered(1))
    small = lambda a: pl.BlockSpec(a.shape, lambda c, b: (0,) * a.ndim)
    fa, fi, w2, v2 = consts["fa_half"], consts["fi"], consts["w2"], consts["v2"]
    return pl.pallas_call(
        partial(_hyena_kernel, L=L),
        grid=(HY_SLABS, B),
        in_specs=[u_spec(0), u_spec(1), u_spec(2), small(sw), small(sb), small(skip), k_spec(0), k_spec(1),
                  small(fa), small(fi), const(w2), const(v2)],
        out_specs=pl.BlockSpec((1, 1, L, LANES), lambda c, b: (b, c, 0, 0)),
        out_shape=jax.ShapeDtypeStruct((B, HY_SLABS, L, LANES), jnp.float32),
        scratch_shapes=[pltpu.VMEM((n_ts * X_PITCH, LANES), jnp.float32),
                        pltpu.VMEM((n_ts * X_PITCH, LANES), jnp.float32),
                        pltpu.VMEM((kp * A_PITCH, LANES), jnp.float32)],
        compiler_params=_compiler_params(("arbitrary", "arbitrary")),
        name="hyena_mix",
    )(u, u, u, sw, sb, skip, kspec, kspec, fa, fi, w2, v2)


def _filter_spectrum_kernel(kern_ref, fa_ref, w2_ref, o_ref, xs_ref, ag_ref, *, M):
    p = DFT_P
    q, ka, kp = _dft_sizes(M)

    def load(ts, carry):
        xs_ref[pl.ds(pl.multiple_of(ts * X_PITCH, 8), p), :] = kern_ref[0, 0, pl.ds(pl.multiple_of(ts * p, p), p), :]
        return carry

    lax.fori_loop(0, q, load, 0)
    _slow_stage_forward(xs_ref, ag_ref, fa_ref, q, kp)

    def body(k, carry):
        _, x2 = _fast_stage(ag_ref, w2_ref, k)
        o_ref[0, 0, pl.ds(pl.multiple_of(k * 2 * p, 2 * p), 2 * p), :] = x2
        return carry

    lax.fori_loop(0, ka, body, 0)


def filter_spectrum(kern, consts):
    n_o, n_c, M, _ = kern.shape
    q, ka, kp = _dft_sizes(M)
    fa, w2 = consts["fa_full"], consts["w2"]
    full = lambda a: pl.BlockSpec(a.shape, lambda o, c: (0,) * a.ndim)
    return pl.pallas_call(
        partial(_filter_spectrum_kernel, M=M),
        grid=(n_o, n_c),
        in_specs=[pl.BlockSpec((1, 1, M, LANES), lambda o, c: (o, c, 0, 0)), full(fa), full(w2)],
        out_specs=pl.BlockSpec((1, 1, ka * 2 * DFT_P, LANES), lambda o, c: (o, c, 0, 0)),
        out_shape=jax.ShapeDtypeStruct((n_o, n_c, ka * 2 * DFT_P, LANES), jnp.float32),
        scratch_shapes=[pltpu.VMEM((q * X_PITCH, LANES), jnp.float32),
                        pltpu.VMEM((kp * A_PITCH, LANES), jnp.float32)],
        compiler_params=_compiler_params(("arbitrary", "arbitrary")),
        name="filter_spectrum",
    )(kern, fa, w2)


FILTER_ROW_TILE = 512


def _filter_kernel(bands_ref, w1t_ref, w1c_ref, w1s_ref, b1_ref, w2_ref, b2_ref, w3_ref, b3_ref, w4_ref, freq_ref,
                   absd_ref, o_ref, *, L):
    tr = o_ref.shape[2]
    hp = lax.Precision.HIGHEST
    rho = pl.program_id(0) * tr + lax.broadcasted_iota(jnp.int32, (tr, 1), 0)
    tau = jnp.where(rho < L, rho, 2 * L - rho).astype(jnp.float32)
    t = tau * (1.0 / (L - 1))
    ang = (2.0 * math.pi * tau / L) * bands_ref[...]
    dot = lambda a, b: jnp.dot(a, b, preferred_element_type=jnp.float32, precision=hp)
    fr = freq_ref[...]
    pre = t * w1t_ref[...] + dot(jnp.cos(ang), w1c_ref[...]) - dot(jnp.sin(ang), w1s_ref[...]) + b1_ref[...]
    h = jnp.sin(fr * pre)
    h = jnp.sin(fr * (dot(h, w2_ref[...]) + b2_ref[...]))
    h = jnp.sin(fr * (dot(h, w3_ref[...]) + b3_ref[...]))
    hh = dot(h, w4_ref[...])
    decay = jnp.exp(-t * absd_ref[...])
    fwd = rho < L
    nonzero = rho != L
    w = HYENA_WIDTH
    for o in range(HYENA_ORDER):
        for c in range(HY_SLABS):
            f = hh[:, (2 * o) * w + c * LANES:(2 * o) * w + (c + 1) * LANES]
            b = hh[:, (2 * o + 1) * w + c * LANES:(2 * o + 1) * w + (c + 1) * LANES]
            val = jnp.where(fwd, f, b) * decay[:, c * LANES:(c + 1) * LANES]
            o_ref[o, c] = jnp.where(nonzero, val, 0.0)


def hyena_kernels(L, w1, b1, w2, b2, w3, b3, w4, freq):
    M = 2 * L
    tr = min(FILTER_ROW_TILE, M)
    bands = np.zeros((1, LANES), np.float32)
    bands[0, :FILTER_BANDS] = np.linspace(1e-4, FILTER_BANDS - 1, FILTER_BANDS, dtype=np.float32)
    pad_rows = lambda a: jnp.zeros((LANES, a.shape[1]), jnp.float32).at[:a.shape[0]].set(a)
    min_decay = math.log(DECAY_TARGET) / FAST_DECAY_PCT
    max_decay = math.log(DECAY_TARGET) / SLOW_DECAY_PCT
    absd = np.abs(np.linspace(min_decay, max_decay, HYENA_WIDTH, dtype=np.float32))[None, :]
    args = [jnp.asarray(bands), w1[0:1], pad_rows(w1[1:1 + FILTER_BANDS]), pad_rows(w1[1 + FILTER_BANDS:]),
            b1[None, :], w2, b2[None, :], w3, b3[None, :], w4, freq[None, :], jnp.asarray(absd)]
    full = lambda a: pl.BlockSpec(a.shape, lambda i: (0,) * a.ndim)
    return pl.pallas_call(
        partial(_filter_kernel, L=L),
        grid=(M // tr,),
        in_specs=[full(a) for a in args],
        out_specs=pl.BlockSpec((HYENA_ORDER, HY_SLABS, tr, LANES), lambda i: (0, 0, i, 0)),
        out_shape=jax.ShapeDtypeStruct((HYENA_ORDER, HY_SLABS, M, LANES), jnp.float32),
        compiler_params=_compiler_params(("parallel",)),
        name="hyena_filter",
    )(*args)


def hyena_operator(u, p):
    L = u.shape[2]
    consts = _dft_constants(2 * L)
    kern = hyena_kernels(L, p["filt_w1"], p["filt_b1"], p["filt_w2"], p["filt_b2"], p["filt_w3"], p["filt_b3"],
                         p["filt_w4"], p["filt_freq"])
    kspec = filter_spectrum(kern, consts)
    sw = p["short_w"].reshape(SHORT_CONV * U_SLABS, LANES)
    sb = p["short_b"].reshape(U_SLABS, LANES)
    skip = p["hyena_skip"].reshape(HYENA_ORDER * HY_SLABS, LANES)
    return hyena_mix(u, sw, sb, skip, kspec, consts)


def expert_choice_ffn(x1, h, affinity, p):
    N, D = h.shape
    E = affinity.shape[1]
    cap = EC_CAPACITY * N // N_EXPERTS
    aff3 = affinity.T.reshape(E, N // ROUTE_CHUNK, ROUTE_CHUNK)
    slot3, rank3 = route_select(aff3, cap)
    idx, gates = route_compact(slot3, aff3, rank3[:, :, 0], cap)
    ye = expert_ffn(h[idx], gates[..., None], p["w_gate"], p["w_up"], p["w_down"])
    t = min(COMBINE_TOKENS, N)
    starts = jnp.concatenate([rank3.reshape(E, N)[:, ::t], jnp.full((E, 1), cap, jnp.int32)], axis=1)
    return combine(x1, slot3.reshape(E, N).T, starts, ye, p["final_g"])


def encoder_trunk(x, p):
    B, L, D = x.shape
    qkv, u = in_proj(x, p["norm_mix_g"], p["w_in"])
    attn = dilated_attention(qkv)
    hy = hyena_operator(u, p)
    x1, h, affinity = out_proj(x, attn, hy, p["attn_out_g"], p["hyena_out_g"], p["w_out"], p["norm_ffn_g"],
                               p["w_router"])
    return expert_choice_ffn(x1, h, affinity, p).reshape(B, L, D)


def kernel(x_prompt, x_sample, norm_mix_g, w_in, short_w, short_b, filt_w1, filt_b1, filt_w2, filt_b2, filt_w3, filt_b3, filt_w4, filt_freq, hyena_skip, attn_out_g, hyena_out_g, w_out, norm_ffn_g, w_router, w_gate, w_up, w_down, final_g):
    bf16 = jnp.bfloat16
    p = {
        "norm_mix_g": norm_mix_g, "w_in": w_in[0].astype(bf16),
        "short_w": short_w[0], "short_b": short_b[0],
        "filt_w1": filt_w1[0], "filt_b1": filt_b1[0], "filt_w2": filt_w2[0], "filt_b2": filt_b2[0],
        "filt_w3": filt_w3[0], "filt_b3": filt_b3[0], "filt_w4": filt_w4[0], "filt_freq": filt_freq[0],
        "hyena_skip": hyena_skip[0], "attn_out_g": attn_out_g, "hyena_out_g": hyena_out_g,
        "w_out": w_out[0].astype(bf16), "norm_ffn_g": norm_ffn_g, "w_router": w_router[0],
        "w_gate": w_gate[0].astype(bf16), "w_up": w_up[0].astype(bf16), "w_down": w_down[0].astype(bf16),
        "final_g": final_g.reshape(1, D_MODEL),
    }
    return (encoder_trunk(x_prompt, p), encoder_trunk(x_sample, p))
```
